```python
import math
import jax, jax.numpy as jnp
from jax import lax
import numpy as np

D_MODEL = 1024
BATCH = 8
SEQ = 4096
DEPTH = 4

CTX_LEN = 256
GRID_W = 64
EPS = 1e-6
N_MOD = 6
A_HEADS = 8
A_DIM = 64
A_QBLOCK = 128
ROPE_THETA = 10000.0
B_HEADS = 8
B_DIM = 128
B_CHUNK = 64
SHORT_CONV = 5
C_WIDTH = 1024
C_BLOCKS = 16
LRU_C = 8.0
N_BRANCH = 3
P_HEADS = 8
P_KEYS = 128
P_EXPERTS = P_KEYS * P_KEYS
P_QDIM = 256
P_TOPK = 16
P_BLOCK = 128

A_QKV = 3 * A_HEADS * 2 * A_DIM
B_QKV = 3 * B_HEADS * B_DIM
IN_SPLITS = (A_QKV, B_QKV, B_HEADS * B_DIM, 2 * B_HEADS, 2 * B_HEADS, C_WIDTH, C_WIDTH, N_BRANCH * D_MODEL)
IN_WIDTH = A_QKV + B_QKV + B_HEADS * B_DIM + 4 * B_HEADS + 2 * C_WIDTH + N_BRANCH * D_MODEL

kernel_name = "hybrid_diffattn_gdn_rglru_peer_prefix_dit"


def rmsnorm(x, g=None):
    xf = x.astype(jnp.float32)
    y = xf * lax.rsqrt(jnp.mean(xf * xf, axis=-1, keepdims=True) + EPS)
    if g is not None:
        y = y * g.astype(jnp.float32)
    return y.astype(x.dtype)


def l2norm(x):
    return x * lax.rsqrt(jnp.sum(x * x, axis=-1, keepdims=True) + EPS)


def adaln(cond, w, b):
    m = (jax.nn.silu(cond) @ w + b)[:, None, :]
    return jnp.split(m, N_MOD, axis=-1)


def modulate(h, shift, scale):
    return h * (1.0 + scale) + shift


def dwconv(x, w, b=None):
    pad = SHORT_CONV // 2
    y = lax.conv_general_dilated(x, w[:, None, :].astype(x.dtype), window_strides=(1,),
                                 padding=[(pad, pad)], dimension_numbers=('NWC', 'WIO', 'NWC'),
                                 feature_group_count=x.shape[-1])
    return y if b is None else y + b.astype(x.dtype)


def split_in(p):
    offs = np.cumsum(IN_SPLITS)[:-1].tolist()
    return jnp.split(p, offs, axis=-1)


def flip_if(t, on, axis):
    return jnp.flip(t, axis=axis) if on else t


def axial_rope_tables(rows, dtype):
    row = jnp.repeat(jnp.arange(rows, dtype=jnp.float32), GRID_W)
    col = jnp.tile(jnp.arange(GRID_W, dtype=jnp.float32), rows)
    nf = A_DIM // 4
    inv = ROPE_THETA ** (-jnp.arange(nf, dtype=jnp.float32) / nf)
    ar, ac = row[:, None] * inv, col[:, None] * inv
    return tuple(t.astype(dtype) for t in (jnp.cos(ar), jnp.sin(ar), jnp.cos(ac), jnp.sin(ac)))


def rope_half(x, cos, sin):
    x1, x2 = jnp.split(x, 2, axis=-1)
    cs, sn = cos[:, None, None, :], sin[:, None, None, :]
    return jnp.concatenate([x1 * cs - x2 * sn, x2 * cs + x1 * sn], axis=-1)


def axial_rope(x, tabs):
    cr, sr, cc, sc = tabs
    xr, xc = jnp.split(x, 2, axis=-1)
    return jnp.concatenate([rope_half(xr, cr, sr), rope_half(xc, cc, sc)], axis=-1)


def diff_attend(q, k, v, lam):
    s = jnp.einsum('bqhcd,bkhcd->bhcqk', q, k).astype(jnp.float32) * (A_DIM ** -0.5)
    p = jax.nn.softmax(s, axis=-1)
    a = p[:, :, 0] - lam * p[:, :, 1]
    return jnp.einsum('bhqk,bkhe->bqhe', a.astype(v.dtype), v)


def diff_attention(qkv_c, qkv_l, rope, layer, need_ctx, qn_g, kn_g, lq1, lk1, lq2, lk2, sub_g):
    lam_init = 0.8 - 0.6 * math.exp(-0.3 * layer)
    f = lambda t: t.astype(jnp.float32)
    lam = jnp.exp(jnp.sum(f(lq1) * f(lk1))) - jnp.exp(jnp.sum(f(lq2) * f(lk2))) + lam_init

    def heads(qkv, rotate):
        q, k, v = jnp.split(qkv, 3, axis=-1)
        bn, tn, _ = q.shape
        q = rmsnorm(q.reshape(bn, tn, A_HEADS, 2, A_DIM), qn_g)
        k = rmsnorm(k.reshape(bn, tn, A_HEADS, 2, A_DIM), kn_g)
        if rotate:
            q, k = axial_rope(q, rope), axial_rope(k, rope)
        return q, k, v.reshape(bn, tn, A_HEADS, 2 * A_DIM)

    qc, kc, vc = heads(qkv_c, False)
    ql, kl, vl = heads(qkv_l, True)
    k_all = jnp.concatenate([kc, kl], axis=1)
    v_all = jnp.concatenate([vc, vl], axis=1)
    bn, tn = ql.shape[:2]
    nblk = tn // A_QBLOCK
    qb = jnp.moveaxis(ql.reshape(bn, nblk, A_QBLOCK, A_HEADS, 2, A_DIM), 1, 0)
    ol = lax.map(lambda qx: diff_attend(qx, k_all, v_all, lam), qb)
    ol = jnp.moveaxis(ol, 0, 1).reshape(bn, tn, A_HEADS, 2 * A_DIM)

    def finish(o):
        return (rmsnorm(o, sub_g) * (1.0 - lam_init)).reshape(o.shape[0], o.shape[1], -1)

    yc = finish(diff_attend(qc, kc, vc, lam)) if need_ctx else None
    return yc, finish(ol)


def gdn_chunk_scan(q, k, v, g, beta, s0):
    bn, hn, tn, _ = q.shape
    n = tn // B_CHUNK
    rs = lambda t: t.reshape(bn, hn, n, B_CHUNK, *t.shape[3:])
    q, k, v, g, beta = rs(q), rs(k), rs(v), rs(g), rs(beta)
    g = jnp.cumsum(g, axis=-1)
    idx = jnp.arange(B_CHUNK)
    lower = idx[:, None] >= idx[None, :]
    strict = idx[:, None] > idx[None, :]
    decay = jnp.exp(jnp.where(lower, g[..., :, None] - g[..., None, :], -jnp.inf))
    kb = k * beta[..., None]
    l_mat = jnp.where(strict, jnp.einsum('bhnid,bhnjd->bhnij', kb, k) * decay, 0.0)
    a_mat = l_mat + jnp.eye(B_CHUNK, dtype=q.dtype)
    u = lax.linalg.triangular_solve(a_mat, v * beta[..., None], left_side=True, lower=True, unit_diagonal=True)
    w = lax.linalg.triangular_solve(a_mat, kb * jnp.exp(g)[..., None], left_side=True, lower=True, unit_diagonal=True)
    qk = jnp.einsum('bhnid,bhnjd->bhnij', q, k) * decay
    qg = q * jnp.exp(g)[..., None]
    g_last = g[..., -1]
    kd = k * jnp.exp(g_last[..., None] - g)[..., None]

    def step(s, xs):
        qk_n, u_n, w_n, qg_n, kd_n, gl_n = xs
        v_new = u_n - jnp.einsum('bhcd,bhde->bhce', w_n, s)
        o_n = jnp.einsum('bhcd,bhde->bhce', qg_n, s) + jnp.einsum('bhij,bhje->bhie', qk_n, v_new)
        s = s * jnp.exp(gl_n)[..., None, None] + jnp.einsum('bhcd,bhce->bhde', kd_n, v_new)
        return s, o_n

    xs = tuple(jnp.moveaxis(t, 2, 0) for t in (qk, u, w, qg, kd, g_last))
    s_fin, o = lax.scan(step, s0, xs)
    return jnp.moveaxis(o, 0, 2).reshape(bn, hn, tn, -1), s_fin


def gated_deltanet(parts_c, parts_l, conv_w, a_log, dt_bias, norm_g):
    def prep(parts):
        qkv, z, b_raw, a_raw = parts
        qkv = jax.nn.silu(dwconv(qkv, conv_w)).astype(jnp.float32)
        bn, tn, _ = qkv.shape
        q, k, v = [jnp.moveaxis(t.reshape(bn, tn, B_HEADS, B_DIM), 1, 2) for t in jnp.split(qkv, 3, axis=-1)]
        q = l2norm(q) * (B_DIM ** -0.5)
        k = l2norm(k)
        beta = jax.nn.sigmoid(b_raw.astype(jnp.float32)).reshape(bn, tn, 2, B_HEADS)
        g = -jnp.exp(a_log.astype(jnp.float32)) * jax.nn.softplus(
            a_raw.astype(jnp.float32).reshape(bn, tn, 2, B_HEADS) + dt_bias.astype(jnp.float32))
        return q, k, v, jnp.transpose(g, (2, 0, 3, 1)), jnp.transpose(beta, (2, 0, 3, 1)), z

    qc, kc, vc, gc, bc, zc = prep(parts_c)
    ql, kl, vl, gl, bl, zl = prep(parts_l)
    s0 = jnp.zeros((qc.shape[0], B_HEADS, B_DIM, B_DIM), jnp.float32)
    oc = jnp.zeros_like(vc)
    ol = jnp.zeros_like(vl)
    for d in range(2):
        fl = lambda t: flip_if(t, d == 1, 2)
        o_c, s_ctx = gdn_chunk_scan(fl(qc), fl(kc), fl(vc), fl(gc[d]), fl(bc[d]), s0)
        o_l, _ = gdn_chunk_scan(fl(ql), fl(kl), fl(vl), fl(gl[d]), fl(bl[d]), s_ctx)
        oc = oc + fl(o_c)
        ol = ol + fl(o_l)

    def finish(o, z):
        o = jnp.moveaxis(o, 1, 2)
        y = rmsnorm(o, norm_g) * jax.nn.silu(z.astype(jnp.float32).reshape(o.shape))
        return y.reshape(o.shape[0], o.shape[1], -1).astype(z.dtype)

    return finish(oc, zc), finish(ol, zl)


def blockdiag(x, w, b):
    xb = x.reshape(*x.shape[:-1], C_BLOCKS, C_WIDTH // C_BLOCKS)
    return jnp.einsum('btnc,nce->btne', xb, w).reshape(x.shape) + b


def linear_scan(a, b, h0):
    b = b.at[:, 0].add(a[:, 0] * h0)

    def combine(lhs, rhs):
        a_l, b_l = lhs
        a_r, b_r = rhs
        return a_l * a_r, a_r * b_l + b_r

    _, h = lax.associative_scan(combine, (a, b), axis=1)
    return h


def rglru_scan(x, w_r, b_r, w_i, b_i, lam, h0):
    f = lambda t: t.astype(jnp.float32)
    r = jax.nn.sigmoid(blockdiag(x, f(w_r), f(b_r)))
    i = jax.nn.sigmoid(blockdiag(x, f(w_i), f(b_i)))
    log_a = -LRU_C * jax.nn.softplus(-f(lam)) * r
    a = jnp.exp(log_a)
    b = jnp.sqrt(-jnp.expm1(2.0 * log_a)) * (i * x)
    return linear_scan(a, b, h0)


def rglru_branch(parts_c, parts_l, conv_w, conv_b, w_r, b_r, w_i, b_i, lam):
    xc_raw, gate_c = parts_c
    xl_raw, gate_l = parts_l
    xc = dwconv(xc_raw, conv_w, conv_b).astype(jnp.float32)
    xl = dwconv(xl_raw, conv_w, conv_b).astype(jnp.float32)
    h0 = jnp.zeros((xc.shape[0], C_WIDTH), jnp.float32)
    hc = jnp.zeros_like(xc)
    hl = jnp.zeros_like(xl)
    for d in range(2):
        fl = lambda t: flip_if(t, d == 1, 1)
        hcd = rglru_scan(fl(xc), w_r[d], b_r[d], w_i[d], b_i[d], lam[d], h0)
        hld = rglru_scan(fl(xl), w_r[d], b_r[d], w_i[d], b_i[d], lam[d], hcd[:, -1])
        hc = hc + fl(hcd)
        hl = hl + fl(hld)
    yc = (hc * jax.nn.gelu(gate_c.astype(jnp.float32))).astype(gate_c.dtype)
    yl = (hl * jax.nn.gelu(gate_l.astype(jnp.float32))).astype(gate_l.dtype)
    return yc, yl


def merge_branches(ys, gate_logits, w_branch, w_out):
    gates = jnp.split(jax.nn.sigmoid(gate_logits.astype(jnp.float32)).astype(gate_logits.dtype), N_BRANCH, axis=-1)
    y = gates[0] * (ys[0] @ w_branch[0])
    for k in range(1, N_BRANCH):
        y = y + gates[k] * (ys[k] @ w_branch[k])
    return y @ w_out


def token_mixer(hc, hl, rope, layer, need_ctx, w_in,
                attn_qn_g, attn_kn_g, lam_q1, lam_k1, lam_q2, lam_k2, attn_sub_g,
                gdn_conv_w, gdn_a_log, gdn_dt_bias, gdn_norm_g,
                lru_conv_w, lru_conv_b, lru_w_r, lru_b_r, lru_w_i, lru_b_i, lru_lambda,
                w_branch, w_out):
    pc = split_in(hc @ w_in)
    pl = split_in(hl @ w_in)
    ya_c, ya_l = diff_attention(pc[0], pl[0], rope, layer, need_ctx, attn_qn_g, attn_kn_g,
                                lam_q1, lam_k1, lam_q2, lam_k2, attn_sub_g)
    yb_c, yb_l = gated_deltanet(pc[1:5], pl[1:5], gdn_conv_w, gdn_a_log, gdn_dt_bias, gdn_norm_g)
    yr_c, yr_l = rglru_branch(pc[5:7], pl[5:7], lru_conv_w, lru_conv_b, lru_w_r, lru_b_r,
                              lru_w_i, lru_b_i, lru_lambda)
    yl = merge_branches((ya_l, yb_l, yr_l), pl[7], w_branch, w_out)
    yc = merge_branches((ya_c, yb_c, yr_c), pc[7], w_branch, w_out) if need_ctx else None
    return yc, yl


def peer_ffn(h, wq, subkeys, u, v):
    ntok, dm = h.shape
    q = rmsnorm((h @ wq).reshape(ntok, P_HEADS, 2, P_QDIM // 2))
    s = jnp.einsum('nhcd,hckd->nhck', q, subkeys).astype(jnp.float32)
    s_top, i_top = lax.top_k(s, P_TOPK)
    cand = (s_top[..., 0, :, None] + s_top[..., 1, None, :]).reshape(ntok, P_HEADS, P_TOPK * P_TOPK)
    cand_idx = (i_top[..., 0, :, None] * P_KEYS + i_top[..., 1, None, :]).reshape(ntok, P_HEADS, P_TOPK * P_TOPK)
    best, pos = lax.top_k(cand, P_TOPK)
    idx = jnp.take_along_axis(cand_idx, pos, axis=-1)
    gate = jax.nn.softmax(best, axis=-1).astype(h.dtype)
    nb = ntok // P_BLOCK

    def block(args):
        hx, ix, gx = args
        act = jax.nn.gelu(jnp.einsum('pd,phkd->phk', hx, jnp.take(u, ix, axis=0))) * gx
        return jnp.einsum('phk,phkd->pd', act, jnp.take(v, ix, axis=0))

    out = lax.map(block, (h.reshape(nb, P_BLOCK, dm),
                          idx.reshape(nb, P_BLOCK, P_HEADS, P_TOPK),
                          gate.reshape(nb, P_BLOCK, P_HEADS, P_TOPK)))
    return out.reshape(ntok, dm)


def setup_inputs(seed: int = 0) -> dict:
    key = jax.random.key(seed)
    ks = iter(jax.random.split(key, 48))
    L, D = DEPTH, D_MODEL

    def nrm(shape, scale):
        return scale * jax.random.normal(next(ks), shape, jnp.float32)

    def unif(shape, lo, hi):
        return jax.random.uniform(next(ks), shape, jnp.float32, minval=lo, maxval=hi)

    a_lru = unif((L, 2, C_WIDTH), 0.9, 0.999) ** (1.0 / LRU_C)
    dt = jnp.exp(unif((L, 2, B_HEADS), math.log(1e-3), math.log(1e-1)))
    blk = C_WIDTH // C_BLOCKS
    return {
        "x": nrm((BATCH, SEQ, D), 1.0),
        "c": nrm((BATCH, D), 1.0),
        "ctx": nrm((BATCH, CTX_LEN, D), 1.0),
        "c_ctx": nrm((D,), 1.0),
        "w_ada": nrm((L, D, N_MOD * D), D ** -0.5),
        "b_ada": nrm((L, N_MOD * D), 0.02),
        "norm1_g": 1.0 + nrm((L, D), 0.02),
        "norm2_g": 1.0 + nrm((L, D), 0.02),
        "w_in": nrm((L, D, IN_WIDTH), D ** -0.5),
        "attn_qn_g": 1.0 + nrm((L, A_DIM), 0.02),
        "attn_kn_g": 1.0 + nrm((L, A_DIM), 0.02),
        "lam_q1": nrm((L, A_DIM), 0.1),
        "lam_k1": nrm((L, A_DIM), 0.1),
        "lam_q2": nrm((L, A_DIM), 0.1),
        "lam_k2": nrm((L, A_DIM), 0.1),
        "attn_sub_g": 1.0 + nrm((L, 2 * A_DIM), 0.02),
        "gdn_conv_w": nrm((L, SHORT_CONV, B_QKV), SHORT_CONV ** -0.5),
        "gdn_a_log": jnp.log(unif((L, 2, B_HEADS), 1.0, 16.0)),
        "gdn_dt_bias": dt + jnp.log(-jnp.expm1(-dt)),
        "gdn_norm_g": 1.0 + nrm((L, B_DIM), 0.02),
        "lru_conv_w": nrm((L, SHORT_CONV, C_WIDTH), SHORT_CONV ** -0.5),
        "lru_conv_b": nrm((L, C_WIDTH), 0.02),
        "lru_w_r": nrm((L, 2, C_BLOCKS, blk, blk), blk ** -0.5),
        "lru_b_r": nrm((L, 2, C_WIDTH), 0.02),
        "lru_w_i": nrm((L, 2, C_BLOCKS, blk, blk), blk ** -0.5),
        "lru_b_i": nrm((L, 2, C_WIDTH), 0.02),
        "lru_lambda": jnp.log(a_lru) - jnp.log1p(-a_lru),
        "w_branch": nrm((L, N_BRANCH, C_WIDTH, D), C_WIDTH ** -0.5),
        "w_out": nrm((L, D, D), D ** -0.5),
        "peer_wq": nrm((L, D, P_HEADS * P_QDIM), D ** -0.5),
        "peer_subkeys": nrm((L, P_HEADS, 2, P_KEYS, P_QDIM // 2), (P_QDIM // 2) ** -0.5),
        "peer_u": nrm((L, P_EXPERTS, D), D ** -0.5),
        "peer_v": nrm((L, P_EXPERTS, D), 0.3),
    }


def reference(x, c, ctx, c_ctx, w_ada, b_ada, norm1_g, norm2_g, w_in,
              attn_qn_g, attn_kn_g, lam_q1, lam_k1, lam_q2, lam_k2, attn_sub_g,
              gdn_conv_w, gdn_a_log, gdn_dt_bias, gdn_norm_g,
              lru_conv_w, lru_conv_b, lru_w_r, lru_b_r, lru_w_i, lru_b_i, lru_lambda,
              w_branch, w_out, peer_wq, peer_subkeys, peer_u, peer_v):
    n_lat = x.shape[1]
    ROWS = n_lat // GRID_W
    rope = axial_rope_tables(ROWS, x.dtype)
    dm = x.shape[-1]
    xl, xc = x, ctx
    for l in range(DEPTH):
        need_ctx = l < DEPTH - 1
        ml = adaln(c, w_ada[l], b_ada[l])
        mc = adaln(c_ctx[None, :], w_ada[l], b_ada[l])
        hl = modulate(rmsnorm(xl, norm1_g[l]), ml[0], ml[1])
        hc = modulate(rmsnorm(xc, norm1_g[l]), mc[0], mc[1])
        yc, yl = token_mixer(hc, hl, rope, l, need_ctx, w_in[l],
                             attn_qn_g[l], attn_kn_g[l], lam_q1[l], lam_k1[l], lam_q2[l], lam_k2[l], attn_sub_g[l],
                             gdn_conv_w[l], gdn_a_log[l], gdn_dt_bias[l], gdn_norm_g[l],
                             lru_conv_w[l], lru_conv_b[l], lru_w_r[l], lru_b_r[l], lru_w_i[l], lru_b_i[l], lru_lambda[l],
                             w_branch[l], w_out[l])
        xl = xl + ml[2] * yl
        hl = modulate(rmsnorm(xl, norm2_g[l]), ml[3], ml[4])
        xl = xl + ml[5] * peer_ffn(hl.reshape(-1, dm), peer_wq[l], peer_subkeys[l], peer_u[l], peer_v[l]).reshape(xl.shape)
        if need_ctx:
            xc = xc + mc[2] * yc
            hc = modulate(rmsnorm(xc, norm2_g[l]), mc[3], mc[4])
            xc = xc + mc[5] * peer_ffn(hc.reshape(-1, dm), peer_wq[l], peer_subkeys[l], peer_u[l], peer_v[l]).reshape(xc.shape)
    return xl
```

```python
import functools
import math

import jax
import jax.numpy as jnp
import numpy as np
from jax import lax
from jax.experimental import pallas as pl
from jax.experimental.pallas import tpu as pltpu

D_MODEL = 1024
DEPTH = 4
GRID_W = 64
EPS = 1e-6
N_MOD = 6
A_HEADS = 8
A_DIM = 64
ROPE_THETA = 10000.0
B_HEADS = 8
B_DIM = 128
B_CHUNK = 64
SHORT_CONV = 5
C_WIDTH = 1024
C_BLOCKS = 16
LRU_C = 8.0
N_BRANCH = 3
P_HEADS = 8
P_KEYS = 128
P_QDIM = 256
P_TOPK = 16
P_BLOCK = 128

A_QKV = 3 * A_HEADS * 2 * A_DIM
B_QKV = 3 * B_HEADS * B_DIM
IN_SPLITS = (A_QKV, B_QKV, B_HEADS * B_DIM, 2 * B_HEADS, 2 * B_HEADS, C_WIDTH, C_WIDTH, N_BRANCH * D_MODEL)

F32 = jnp.float32
BF16 = jnp.bfloat16


def _matmul_kernel(x_ref, w_ref, o_ref):
    o_ref[...] = jnp.dot(x_ref[...].astype(BF16), w_ref[...].astype(BF16),
                         preferred_element_type=F32).astype(o_ref.dtype)


def matmul(x, w, *, tm=1024, tn=512, out_dtype=F32):
    m, k = x.shape
    n = w.shape[1]
    tm, tn = min(tm, m), min(tn, n)
    assert m % tm == 0 and n % tn == 0, (m, n, tm, tn)
    return pl.pallas_call(
        _matmul_kernel,
        grid=(m // tm, n // tn),
        in_specs=[pl.BlockSpec((tm, k), lambda i, j: (i, 0)),
                  pl.BlockSpec((k, tn), lambda i, j: (0, j))],
        out_specs=pl.BlockSpec((tm, tn), lambda i, j: (i, j)),
        out_shape=jax.ShapeDtypeStruct((m, n), out_dtype),
        compiler_params=pltpu.CompilerParams(dimension_semantics=("parallel", "parallel")),
        name="matmul",
    )(x, w)


def matmul3(x, w, **kw):
    b, t, k = x.shape
    return matmul(x.reshape(b * t, k), w, **kw).reshape(b, t, -1)


def _diff_attn_kernel(lam_ref, q_ref, kt_ref, v_ref, g_ref, o_ref, *, out_scale):
    q = q_ref[0]
    kt = kt_ref[0]
    v = v_ref[0]
    lane = lax.broadcasted_iota(jnp.int32, q.shape, 1)
    zero = jnp.zeros_like(q)

    def softmax_v(qc):
        s = jnp.dot(qc, kt, preferred_element_type=F32)
        p = jnp.exp(s - jnp.max(s, axis=-1, keepdims=True))
        l = jnp.sum(p, axis=-1, keepdims=True)
        return jnp.dot(p.astype(BF16), v, preferred_element_type=F32) / l

    o = softmax_v(jnp.where(lane < A_DIM, q, zero)) - lam_ref[0] * softmax_v(jnp.where(lane >= A_DIM, q, zero))
    y = o * lax.rsqrt(jnp.mean(o * o, axis=-1, keepdims=True) + EPS) * g_ref[...]
    o_ref[0] = y * out_scale


def diff_attention_core(q, kt, v, lam, sub_g, out_scale, *, tq=256):
    bn, tq_all, _ = q.shape
    tk = v.shape[1]
    tq = min(tq, tq_all)
    hd = 2 * A_DIM
    return pl.pallas_call(
        functools.partial(_diff_attn_kernel, out_scale=out_scale),
        grid=(bn, A_HEADS, tq_all // tq),
        in_specs=[pl.BlockSpec(memory_space=pltpu.SMEM),
                  pl.BlockSpec((1, tq, hd), lambda b, h, i: (b, i, h)),
                  pl.BlockSpec((1, hd, tk), lambda b, h, i: (b, h, 0)),
                  pl.BlockSpec((1, tk, hd), lambda b, h, i: (b, 0, h)),
                  pl.BlockSpec((1, hd), lambda b, h, i: (0, 0))],
        out_specs=pl.BlockSpec((1, tq, hd), lambda b, h, i: (b, i, h)),
        out_shape=jax.ShapeDtypeStruct((bn, tq_all, A_HEADS * hd), F32),
        compiler_params=pltpu.CompilerParams(dimension_semantics=("parallel", "parallel", "parallel")),
        name="diff_attn",
    )(lam.reshape(1).astype(F32), q, kt, v, sub_g.reshape(1, hd).astype(F32))


def rmsnorm(x, g=None):
    xf = x.astype(F32)
    y = xf * lax.rsqrt(jnp.mean(xf * xf, axis=-1, keepdims=True) + EPS)
    if g is not None:
        y = y * g.astype(F32)
    return y.astype(x.dtype)


def l2norm(x):
    return x * lax.rsqrt(jnp.sum(x * x, axis=-1, keepdims=True) + EPS)


def adaln(cond, w, b):
    n = cond.shape[0]
    pad = (-n) % 8
    cp = jnp.pad(jax.nn.silu(cond), ((0, pad), (0, 0)))
    m = (matmul(cp, w)[:n] + b)[:, None, :]
    return jnp.split(m, N_MOD, axis=-1)


def modulate(h, shift, scale):
    return h * (1.0 + scale) + shift


def dwconv(x, w, b=None):
    pad = SHORT_CONV // 2
    y = lax.conv_general_dilated(x, w[:, None, :].astype(x.dtype), window_strides=(1,),
                                 padding=[(pad, pad)], dimension_numbers=('NWC', 'WIO', 'NWC'),
                                 feature_group_count=x.shape[-1])
    return y if b is None else y + b.astype(x.dtype)


def split_in(p):
    offs = np.cumsum(IN_SPLITS)[:-1].tolist()
    return jnp.split(p, offs, axis=-1)


def flip_if(t, on, axis):
    return jnp.flip(t, axis=axis) if on else t


def axial_rope_tables(rows, dtype):
    row = jnp.repeat(jnp.arange(rows, dtype=F32), GRID_W)
    col = jnp.tile(jnp.arange(GRID_W, dtype=F32), rows)
    nf = A_DIM // 4
    inv = ROPE_THETA ** (-jnp.arange(nf, dtype=F32) / nf)
    ar, ac = row[:, None] * inv, col[:, None] * inv
    return tuple(t.astype(dtype) for t in (jnp.cos(ar), jnp.sin(ar), jnp.cos(ac), jnp.sin(ac)))


def rope_half(x, cos, sin):
    x1, x2 = jnp.split(x, 2, axis=-1)
    cs, sn = cos[:, None, None, :], sin[:, None, None, :]
    return jnp.concatenate([x1 * cs - x2 * sn, x2 * cs + x1 * sn], axis=-1)


def axial_rope(x, tabs):
    cr, sr, cc, sc = tabs
    xr, xc = jnp.split(x, 2, axis=-1)
    return jnp.concatenate([rope_half(xr, cr, sr), rope_half(xc, cc, sc)], axis=-1)


def diff_attention(qkv_c, qkv_l, rope, layer, need_ctx, qn_g, kn_g, lq1, lk1, lq2, lk2, sub_g):
    lam_init = 0.8 - 0.6 * math.exp(-0.3 * layer)
    f = lambda t: t.astype(F32)
    lam = jnp.exp(jnp.sum(f(lq1) * f(lk1))) - jnp.exp(jnp.sum(f(lq2) * f(lk2))) + lam_init

    def heads(qkv, rotate):
        q, k, v = jnp.split(qkv, 3, axis=-1)
        bn, tn, _ = q.shape
        q = rmsnorm(q.reshape(bn, tn, A_HEADS, 2, A_DIM), qn_g)
        k = rmsnorm(k.reshape(bn, tn, A_HEADS, 2, A_DIM), kn_g)
        if rotate:
            q, k = axial_rope(q, rope), axial_rope(k, rope)
        q = (q * (A_DIM ** -0.5)).reshape(bn, tn, -1).astype(BF16)
        return q, k.reshape(bn, tn, -1).astype(BF16), v.astype(BF16)

    qc, kc, vc = heads(qkv_c, False)
    ql, kl, vl = heads(qkv_l, True)
    k_all = jnp.concatenate([kc, kl], axis=1)
    v_all = jnp.concatenate([vc, vl], axis=1)
    scale = 1.0 - lam_init
    yl = diff_attention_core(ql, jnp.swapaxes(k_all, 1, 2), v_all, lam, sub_g, scale)
    yc = diff_attention_core(qc, jnp.swapaxes(kc, 1, 2), vc, lam, sub_g, scale) if need_ctx else None
    return yc, yl


def gdn_chunk_scan(q, k, v, g, beta, s0):
    bn, hn, tn, _ = q.shape
    n = tn // B_CHUNK
    rs = lambda t: t.reshape(bn, hn, n, B_CHUNK, *t.shape[3:])
    q, k, v, g, beta = rs(q), rs(k), rs(v), rs(g), rs(beta)
    g = jnp.cumsum(g, axis=-1)
    idx = jnp.arange(B_CHUNK)
    lower = idx[:, None] >= idx[None, :]
    strict = idx[:, None] > idx[None, :]
    decay = jnp.exp(jnp.where(lower, g[..., :, None] - g[..., None, :], -jnp.inf))
    kb = k * beta[..., None]
    l_mat = jnp.where(strict, jnp.einsum('bhnid,bhnjd->bhnij', kb, k) * decay, 0.0)
    a_mat = l_mat + jnp.eye(B_CHUNK, dtype=q.dtype)
    u = lax.linalg.triangular_solve(a_mat, v * beta[..., None], left_side=True, lower=True, unit_diagonal=True)
    w = lax.linalg.triangular_solve(a_mat, kb * jnp.exp(g)[..., None], left_side=True, lower=True, unit_diagonal=True)
    qk = jnp.einsum('bhnid,bhnjd->bhnij', q, k) * decay
    qg = q * jnp.exp(g)[..., None]
    g_last = g[..., -1]
    kd = k * jnp.exp(g_last[..., None] - g)[..., None]

    def step(s, xs):
        qk_n, u_n, w_n, qg_n, kd_n, gl_n = xs
        v_new = u_n - jnp.einsum('bhcd,bhde->bhce', w_n, s)
        o_n = jnp.einsum('bhcd,bhde->bhce', qg_n, s) + jnp.einsum('bhij,bhje->bhie', qk_n, v_new)
        s = s * jnp.exp(gl_n)[..., None, None] + jnp.einsum('bhcd,bhce->bhde', kd_n, v_new)
        return s, o_n

    xs = tuple(jnp.moveaxis(t, 2, 0) for t in (qk, u, w, qg, kd, g_last))
    s_fin, o = lax.scan(step, s0, xs)
    return jnp.moveaxis(o, 0, 2).reshape(bn, hn, tn, -1), s_fin


def gated_deltanet(parts_c, parts_l, conv_w, a_log, dt_bias, norm_g):
    def prep(parts):
        qkv, z, b_raw, a_raw = parts
        qkv = jax.nn.silu(dwconv(qkv, conv_w)).astype(F32)
        bn, tn, _ = qkv.shape
        q, k, v = [jnp.moveaxis(t.reshape(bn, tn, B_HEADS, B_DIM), 1, 2) for t in jnp.split(qkv, 3, axis=-1)]
        q = l2norm(q) * (B_DIM ** -0.5)
        k = l2norm(k)
        beta = jax.nn.sigmoid(b_raw.astype(F32)).reshape(bn, tn, 2, B_HEADS)
        g = -jnp.exp(a_log.astype(F32)) * jax.nn.softplus(
            a_raw.astype(F32).reshape(bn, tn, 2, B_HEADS) + dt_bias.astype(F32))
        return q, k, v, jnp.transpose(g, (2, 0, 3, 1)), jnp.transpose(beta, (2, 0, 3, 1)), z

    qc, kc, vc, gc, bc, zc = prep(parts_c)
    ql, kl, vl, gl, bl, zl = prep(parts_l)
    s0 = jnp.zeros((qc.shape[0], B_HEADS, B_DIM, B_DIM), F32)
    oc = jnp.zeros_like(vc)
    ol = jnp.zeros_like(vl)
    for d in range(2):
        fl = lambda t: flip_if(t, d == 1, 2)
        o_c, s_ctx = gdn_chunk_scan(fl(qc), fl(kc), fl(vc), fl(gc[d]), fl(bc[d]), s0)
        o_l, _ = gdn_chunk_scan(fl(ql), fl(kl), fl(vl), fl(gl[d]), fl(bl[d]), s_ctx)
        oc = oc + fl(o_c)
        ol = ol + fl(o_l)

    def finish(o, z):
        o = jnp.moveaxis(o, 1, 2)
        y = rmsnorm(o, norm_g) * jax.nn.silu(z.astype(F32).reshape(o.shape))
        return y.reshape(o.shape[0], o.shape[1], -1).astype(z.dtype)

    return finish(oc, zc), finish(ol, zl)


def blockdiag(x, w, b):
    xb = x.reshape(*x.shape[:-1], C_BLOCKS, C_WIDTH // C_BLOCKS)
    return jnp.einsum('btnc,nce->btne', xb, w).reshape(x.shape) + b


def linear_scan(a, b, h0):
    b = b.at[:, 0].add(a[:, 0] * h0)

    def combine(lhs, rhs):
        a_l, b_l = lhs
        a_r, b_r = rhs
        return a_l * a_r, a_r * b_l + b_r

    _, h = lax.associative_scan(combine, (a, b), axis=1)
    return h


def rglru_scan(x, w_r, b_r, w_i, b_i, lam, h0):
    f = lambda t: t.astype(F32)
    r = jax.nn.sigmoid(blockdiag(x, f(w_r), f(b_r)))
    i = jax.nn.sigmoid(blockdiag(x, f(w_i), f(b_i)))
    log_a = -LRU_C * jax.nn.softplus(-f(lam)) * r
    a = jnp.exp(log_a)
    b = jnp.sqrt(-jnp.expm1(2.0 * log_a)) * (i * x)
    return linear_scan(a, b, h0)


def rglru_branch(parts_c, parts_l, conv_w, conv_b, w_r, b_r, w_i, b_i, lam):
    xc_raw, gate_c = parts_c
    xl_raw, gate_l = parts_l
    xc = dwconv(xc_raw, conv_w, conv_b).astype(F32)
    xl = dwconv(xl_raw, conv_w, conv_b).astype(F32)
    h0 = jnp.zeros((xc.shape[0], C_WIDTH), F32)
    hc = jnp.zeros_like(xc)
    hl = jnp.zeros_like(xl)
    for d in range(2):
        fl = lambda t: flip_if(t, d == 1, 1)
        hcd = rglru_scan(fl(xc), w_r[d], b_r[d], w_i[d], b_i[d], lam[d], h0)
        hld = rglru_scan(fl(xl), w_r[d], b_r[d], w_i[d], b_i[d], lam[d], hcd[:, -1])
        hc = hc + fl(hcd)
        hl = hl + fl(hld)
    yc = (hc * jax.nn.gelu(gate_c.astype(F32))).astype(gate_c.dtype)
    yl = (hl * jax.nn.gelu(gate_l.astype(F32))).astype(gate_l.dtype)
    return yc, yl


def merge_branches(ys, gate_logits, w_branch, w_out):
    gates = jnp.split(jax.nn.sigmoid(gate_logits.astype(F32)), N_BRANCH, axis=-1)
    y = gates[0] * matmul3(ys[0], w_branch[0])
    for k in range(1, N_BRANCH):
        y = y + gates[k] * matmul3(ys[k], w_branch[k])
    return matmul3(y, w_out)


def token_mixer(hc, hl, rope, layer, need_ctx, w_in,
                attn_qn_g, attn_kn_g, lam_q1, lam_k1, lam_q2, lam_k2, attn_sub_g,
                gdn_conv_w, gdn_a_log, gdn_dt_bias, gdn_norm_g,
                lru_conv_w, lru_conv_b, lru_w_r, lru_b_r, lru_w_i, lru_b_i, lru_lambda,
                w_branch, w_out):
    offs = np.cumsum((0,) + IN_SPLITS)
    big = [0, 1, 2, 5, 6, 7]
    w_big = jnp.concatenate([w_in[:, offs[i]:offs[i + 1]] for i in big], axis=1).astype(BF16)
    w_small = jnp.pad(w_in[:, offs[3]:offs[5]], ((0, 0), (0, 128 - 4 * B_HEADS))).astype(BF16)
    big_offs = np.cumsum([IN_SPLITS[i] for i in big])[:-1].tolist()

    def project(h):
        pb = jnp.split(matmul3(h, w_big), big_offs, axis=-1)
        ps = matmul3(h, w_small, tn=128)
        return [pb[0], pb[1], pb[2], ps[..., :2 * B_HEADS], ps[..., 2 * B_HEADS:4 * B_HEADS], pb[3], pb[4], pb[5]]

    pc = project(hc)
    pl_ = project(hl)
    ya_c, ya_l = diff_attention(pc[0], pl_[0], rope, layer, need_ctx, attn_qn_g, attn_kn_g,
                                lam_q1, lam_k1, lam_q2, lam_k2, attn_sub_g)
    yb_c, yb_l = gated_deltanet(pc[1:5], pl_[1:5], gdn_conv_w, gdn_a_log, gdn_dt_bias, gdn_norm_g)
    yr_c, yr_l = rglru_branch(pc[5:7], pl_[5:7], lru_conv_w, lru_conv_b, lru_w_r, lru_b_r,
                              lru_w_i, lru_b_i, lru_lambda)
    wb = w_branch.astype(BF16)
    wo = w_out.astype(BF16)
    yl = merge_branches((ya_l, yb_l, yr_l), pl_[7], wb, wo)
    yc = merge_branches((ya_c, yb_c, yr_c), pc[7], wb, wo) if need_ctx else None
    return yc, yl


def peer_ffn(h, wq, subkeys, u, v):
    ntok, dm = h.shape
    q = rmsnorm(matmul(h, wq.astype(BF16)).reshape(ntok, P_HEADS, 2, P_QDIM // 2))
    s = jnp.einsum('nhcd,hckd->nhck', q, subkeys).astype(F32)
    s_top, i_top = lax.top_k(s, P_TOPK)
    cand = (s_top[..., 0, :, None] + s_top[..., 1, None, :]).reshape(ntok, P_HEADS, P_TOPK * P_TOPK)
    cand_idx = (i_top[..., 0, :, None] * P_KEYS + i_top[..., 1, None, :]).reshape(ntok, P_HEADS, P_TOPK * P_TOPK)
    best, pos = lax.top_k(cand, P_TOPK)
    idx = jnp.take_along_axis(cand_idx, pos, axis=-1)
    gate = jax.nn.softmax(best, axis=-1).astype(h.dtype)
    nb = ntok // P_BLOCK

    def block(args):
        hx, ix, gx = args
        act = jax.nn.gelu(jnp.einsum('pd,phkd->phk', hx, jnp.take(u, ix, axis=0))) * gx
        return jnp.einsum('phk,phkd->pd', act, jnp.take(v, ix, axis=0))

    out = lax.map(block, (h.reshape(nb, P_BLOCK, dm),
                          idx.reshape(nb, P_BLOCK, P_HEADS, P_TOPK),
                          gate.reshape(nb, P_BLOCK, P_HEADS, P_TOPK)))
    return out.reshape(ntok, dm)


def kernel(x, c, ctx, c_ctx, w_ada, b_ada, norm1_g, norm2_g, w_in, attn_qn_g, attn_kn_g, lam_q1, lam_k1, lam_q2, lam_k2, attn_sub_g, gdn_conv_w, gdn_a_log, gdn_dt_bias, gdn_norm_g, lru_conv_w, lru_conv_b, lru_w_r, lru_b_r, lru_w_i, lru_b_i, lru_lambda, w_branch, w_out, peer_wq, peer_subkeys, peer_u, peer_v):
    n_lat = x.shape[1]
    rope = axial_rope_tables(n_lat // GRID_W, x.dtype)
    dm = x.shape[-1]
    xl, xc = x, ctx
    for l in range(DEPTH):
        need_ctx = l < DEPTH - 1
        w_ada_l = w_ada[l].astype(BF16)
        ml = adaln(c, w_ada_l, b_ada[l])
        mc = adaln(c_ctx[None, :], w_ada_l, b_ada[l])
        hl = modulate(rmsnorm(xl, norm1_g[l]), ml[0], ml[1])
        hc = modulate(rmsnorm(xc, norm1_g[l]), mc[0], mc[1])
        yc, yl = token_mixer(hc, hl, rope, l, need_ctx, w_in[l],
                             attn_qn_g[l], attn_kn_g[l], lam_q1[l], lam_k1[l], lam_q2[l], lam_k2[l], attn_sub_g[l],
                             gdn_conv_w[l], gdn_a_log[l], gdn_dt_bias[l], gdn_norm_g[l],
                             lru_conv_w[l], lru_conv_b[l], lru_w_r[l], lru_b_r[l], lru_w_i[l], lru_b_i[l], lru_lambda[l],
                             w_branch[l], w_out[l])
        xl = xl + ml[2] * yl
        hl = modulate(rmsnorm(xl, norm2_g[l]), ml[3], ml[4])
        xl = xl + ml[5] * peer_ffn(hl.reshape(-1, dm), peer_wq[l], peer_subkeys[l], peer_u[l], peer_v[l]).reshape(xl.shape)
        if need_ctx:
            xc = xc + mc[2] * yc
            hc = modulate(rmsnorm(xc, norm2_g[l]), mc[3], mc[4])
            xc = xc + mc[5] * peer_ffn(hc.reshape(-1, dm), peer_wq[l], peer_subkeys[l], peer_u[l], peer_v[l]).reshape(xc.shape)
    return xl
```

```python
import functools
import math

import jax
import jax.numpy as jnp
import numpy as np
from jax import lax
from jax.experimental import pallas as pl
from jax.experimental.pallas import tpu as pltpu

D_MODEL = 1024
DEPTH = 4
GRID_W = 64
EPS = 1e-6
N_MOD = 6
A_HEADS = 8
A_DIM = 64
ROPE_THETA = 10000.0
B_HEADS = 8
B_DIM = 128
B_CHUNK = 64
SHORT_CONV = 5
C_WIDTH = 1024
C_BLOCKS = 16
LRU_C = 8.0
N_BRANCH = 3
P_HEADS = 8
P_KEYS = 128
P_QDIM = 256
P_TOPK = 16
P_BLOCK = 128

A_QKV = 3 * A_HEADS * 2 * A_DIM
B_QKV = 3 * B_HEADS * B_DIM
IN_SPLITS = (A_QKV, B_QKV, B_HEADS * B_DIM, 2 * B_HEADS, 2 * B_HEADS, C_WIDTH, C_WIDTH, N_BRANCH * D_MODEL)

F32 = jnp.float32
BF16 = jnp.bfloat16


def _matmul_kernel(x_ref, w_ref, o_ref):
    o_ref[...] = jnp.dot(x_ref[...].astype(BF16), w_ref[...].astype(BF16),
                         preferred_element_type=F32).astype(o_ref.dtype)


def matmul(x, w, *, tm=1024, tn=512, out_dtype=F32):
    m, k = x.shape
    n = w.shape[1]
    tm, tn = min(tm, m), min(tn, n)
    assert m % tm == 0 and n % tn == 0, (m, n, tm, tn)
    return pl.pallas_call(
        _matmul_kernel,
        grid=(m // tm, n // tn),
        in_specs=[pl.BlockSpec((tm, k), lambda i, j: (i, 0)),
                  pl.BlockSpec((k, tn), lambda i, j: (0, j))],
        out_specs=pl.BlockSpec((tm, tn), lambda i, j: (i, j)),
        out_shape=jax.ShapeDtypeStruct((m, n), out_dtype),
        compiler_params=pltpu.CompilerParams(dimension_semantics=("parallel", "parallel")),
        name="matmul",
    )(x, w)


def matmul3(x, w, **kw):
    b, t, k = x.shape
    return matmul(x.reshape(b * t, k), w, **kw).reshape(b, t, -1)


def _diff_attn_kernel(lam_ref, q_ref, kt_ref, v_ref, g_ref, o_ref, *, out_scale):
    q = q_ref[0]
    kt = kt_ref[0]
    v = v_ref[0]
    lane = lax.broadcasted_iota(jnp.int32, q.shape, 1)
    zero = jnp.zeros_like(q)

    def softmax_v(qc):
        s = jnp.dot(qc, kt, preferred_element_type=F32)
        p = jnp.exp(s - jnp.max(s, axis=-1, keepdims=True))
        l = jnp.sum(p, axis=-1, keepdims=True)
        return jnp.dot(p.astype(BF16), v, preferred_element_type=F32) / l

    o = softmax_v(jnp.where(lane < A_DIM, q, zero)) - lam_ref[0] * softmax_v(jnp.where(lane >= A_DIM, q, zero))
    y = o * lax.rsqrt(jnp.mean(o * o, axis=-1, keepdims=True) + EPS) * g_ref[...]
    o_ref[0] = y * out_scale


def diff_attention_core(q, kt, v, lam, sub_g, out_scale, *, tq=256):
    bn, tq_all, _ = q.shape
    tk = v.shape[1]
    tq = min(tq, tq_all)
    hd = 2 * A_DIM
    return pl.pallas_call(
        functools.partial(_diff_attn_kernel, out_scale=out_scale),
        grid=(bn, A_HEADS, tq_all // tq),
        in_specs=[pl.BlockSpec(memory_space=pltpu.SMEM),
                  pl.BlockSpec((1, tq, hd), lambda b, h, i: (b, i, h)),
                  pl.BlockSpec((1, hd, tk), lambda b, h, i: (b, h, 0)),
                  pl.BlockSpec((1, tk, hd), lambda b, h, i: (b, 0, h)),
                  pl.BlockSpec((1, hd), lambda b, h, i: (0, 0))],
        out_specs=pl.BlockSpec((1, tq, hd), lambda b, h, i: (b, i, h)),
        out_shape=jax.ShapeDtypeStruct((bn, tq_all, A_HEADS * hd), F32),
        compiler_params=pltpu.CompilerParams(dimension_semantics=("parallel", "parallel", "parallel")),
        name="diff_attn",
    )(lam.reshape(1).astype(F32), q, kt, v, sub_g.reshape(1, hd).astype(F32))


def rmsnorm(x, g=None):
    xf = x.astype(F32)
    y = xf * lax.rsqrt(jnp.mean(xf * xf, axis=-1, keepdims=True) + EPS)
    if g is not None:
        y = y * g.astype(F32)
    return y.astype(x.dtype)


def l2norm(x):
    return x * lax.rsqrt(jnp.sum(x * x, axis=-1, keepdims=True) + EPS)


def adaln(cond, w, b):
    n = cond.shape[0]
    pad = (-n) % 8
    cp = jnp.pad(jax.nn.silu(cond), ((0, pad), (0, 0)))
    m = (matmul(cp, w)[:n] + b)[:, None, :]
    return jnp.split(m, N_MOD, axis=-1)


def modulate(h, shift, scale):
    return h * (1.0 + scale) + shift


def dwconv(x, w, b=None):
    pad = SHORT_CONV // 2
    y = lax.conv_general_dilated(x, w[:, None, :].astype(x.dtype), window_strides=(1,),
                                 padding=[(pad, pad)], dimension_numbers=('NWC', 'WIO', 'NWC'),
                                 feature_group_count=x.shape[-1])
    return y if b is None else y + b.astype(x.dtype)


def split_in(p):
    offs = np.cumsum(IN_SPLITS)[:-1].tolist()
    return jnp.split(p, offs, axis=-1)


def flip_if(t, on, axis):
    return jnp.flip(t, axis=axis) if on else t


def axial_rope_tables(rows, dtype):
    row = jnp.repeat(jnp.arange(rows, dtype=F32), GRID_W)
    col = jnp.tile(jnp.arange(GRID_W, dtype=F32), rows)
    nf = A_DIM // 4
    inv = ROPE_THETA ** (-jnp.arange(nf, dtype=F32) / nf)
    ar, ac = row[:, None] * inv, col[:, None] * inv
    return tuple(t.astype(dtype) for t in (jnp.cos(ar), jnp.sin(ar), jnp.cos(ac), jnp.sin(ac)))


def rope_half(x, cos, sin):
    x1, x2 = jnp.split(x, 2, axis=-1)
    cs, sn = cos[:, None, None, :], sin[:, None, None, :]
    return jnp.concatenate([x1 * cs - x2 * sn, x2 * cs + x1 * sn], axis=-1)


def axial_rope(x, tabs):
    cr, sr, cc, sc = tabs
    xr, xc = jnp.split(x, 2, axis=-1)
    return jnp.concatenate([rope_half(xr, cr, sr), rope_half(xc, cc, sc)], axis=-1)


def diff_attention(qkv_c, qkv_l, rope, layer, need_ctx, qn_g, kn_g, lq1, lk1, lq2, lk2, sub_g):
    lam_init = 0.8 - 0.6 * math.exp(-0.3 * layer)
    f = lambda t: t.astype(F32)
    lam = jnp.exp(jnp.sum(f(lq1) * f(lk1))) - jnp.exp(jnp.sum(f(lq2) * f(lk2))) + lam_init

    def heads(qkv, rotate):
        q, k, v = jnp.split(qkv, 3, axis=-1)
        bn, tn, _ = q.shape
        q = rmsnorm(q.reshape(bn, tn, A_HEADS, 2, A_DIM), qn_g)
        k = rmsnorm(k.reshape(bn, tn, A_HEADS, 2, A_DIM), kn_g)
        if rotate:
            q, k = axial_rope(q, rope), axial_rope(k, rope)
        q = (q * (A_DIM ** -0.5)).reshape(bn, tn, -1).astype(BF16)
        return q, k.reshape(bn, tn, -1).astype(BF16), v.astype(BF16)

    qc, kc, vc = heads(qkv_c, False)
    ql, kl, vl = heads(qkv_l, True)
    k_all = jnp.concatenate([kc, kl], axis=1)
    v_all = jnp.concatenate([vc, vl], axis=1)
    scale = 1.0 - lam_init
    yl = diff_attention_core(ql, jnp.swapaxes(k_all, 1, 2), v_all, lam, sub_g, scale)
    yc = diff_attention_core(qc, jnp.swapaxes(kc, 1, 2), vc, lam, sub_g, scale) if need_ctx else None
    return yc, yl


def gdn_chunk_scan(q, k, v, g, beta, s0):
    bn, hn, tn, _ = q.shape
    n = tn // B_CHUNK
    rs = lambda t: t.reshape(bn, hn, n, B_CHUNK, *t.shape[3:])
    q, k, v, g, beta = rs(q), rs(k), rs(v), rs(g), rs(beta)
    g = jnp.cumsum(g, axis=-1)
    idx = jnp.arange(B_CHUNK)
    lower = idx[:, None] >= idx[None, :]
    strict = idx[:, None] > idx[None, :]
    decay = jnp.exp(jnp.where(lower, g[..., :, None] - g[..., None, :], -jnp.inf))
    kb = k * beta[..., None]
    l_mat = jnp.where(strict, jnp.einsum('bhnid,bhnjd->bhnij', kb, k) * decay, 0.0)
    a_mat = l_mat + jnp.eye(B_CHUNK, dtype=q.dtype)
    u = lax.linalg.triangular_solve(a_mat, v * beta[..., None], left_side=True, lower=True, unit_diagonal=True)
    w = lax.linalg.triangular_solve(a_mat, kb * jnp.exp(g)[..., None], left_side=True, lower=True, unit_diagonal=True)
    qk = jnp.einsum('bhnid,bhnjd->bhnij', q, k) * decay
    qg = q * jnp.exp(g)[..., None]
    g_last = g[..., -1]
    kd = k * jnp.exp(g_last[..., None] - g)[..., None]

    def step(s, xs):
        qk_n, u_n, w_n, qg_n, kd_n, gl_n = xs
        v_new = u_n - jnp.einsum('bhcd,bhde->bhce', w_n, s)
        o_n = jnp.einsum('bhcd,bhde->bhce', qg_n, s) + jnp.einsum('bhij,bhje->bhie', qk_n, v_new)
        s = s * jnp.exp(gl_n)[..., None, None] + jnp.einsum('bhcd,bhce->bhde', kd_n, v_new)
        return s, o_n

    xs = tuple(jnp.moveaxis(t, 2, 0) for t in (qk, u, w, qg, kd, g_last))
    s_fin, o = lax.scan(step, s0, xs)
    return jnp.moveaxis(o, 0, 2).reshape(bn, hn, tn, -1), s_fin


def gated_deltanet(parts_c, parts_l, conv_w, a_log, dt_bias, norm_g):
    def prep(parts):
        qkv, z, b_raw, a_raw = parts
        qkv = jax.nn.silu(dwconv(qkv, conv_w)).astype(F32)
        bn, tn, _ = qkv.shape
        q, k, v = [jnp.moveaxis(t.reshape(bn, tn, B_HEADS, B_DIM), 1, 2) for t in jnp.split(qkv, 3, axis=-1)]
        q = l2norm(q) * (B_DIM ** -0.5)
        k = l2norm(k)
        beta = jax.nn.sigmoid(b_raw.astype(F32)).reshape(bn, tn, 2, B_HEADS)
        g = -jnp.exp(a_log.astype(F32)) * jax.nn.softplus(
            a_raw.astype(F32).reshape(bn, tn, 2, B_HEADS) + dt_bias.astype(F32))
        return q, k, v, jnp.transpose(g, (2, 0, 3, 1)), jnp.transpose(beta, (2, 0, 3, 1)), z

    qc, kc, vc, gc, bc, zc = prep(parts_c)
    ql, kl, vl, gl, bl, zl = prep(parts_l)
    s0 = jnp.zeros((qc.shape[0], B_HEADS, B_DIM, B_DIM), F32)
    oc = jnp.zeros_like(vc)
    ol = jnp.zeros_like(vl)
    for d in range(2):
        fl = lambda t: flip_if(t, d == 1, 2)
        o_c, s_ctx = gdn_chunk_scan(fl(qc), fl(kc), fl(vc), fl(gc[d]), fl(bc[d]), s0)
        o_l, _ = gdn_chunk_scan(fl(ql), fl(kl), fl(vl), fl(gl[d]), fl(bl[d]), s_ctx)
        oc = oc + fl(o_c)
        ol = ol + fl(o_l)

    def finish(o, z):
        o = jnp.moveaxis(o, 1, 2)
        y = rmsnorm(o, norm_g) * jax.nn.silu(z.astype(F32).reshape(o.shape))
        return y.reshape(o.shape[0], o.shape[1], -1).astype(z.dtype)

    return finish(oc, zc), finish(ol, zl)


def blockdiag(x, w, b):
    xb = x.reshape(*x.shape[:-1], C_BLOCKS, C_WIDTH // C_BLOCKS)
    return jnp.einsum('btnc,nce->btne', xb, w).reshape(x.shape) + b


def linear_scan(a, b, h0):
    b = b.at[:, 0].add(a[:, 0] * h0)

    def combine(lhs, rhs):
        a_l, b_l = lhs
        a_r, b_r = rhs
        return a_l * a_r, a_r * b_l + b_r

    _, h = lax.associative_scan(combine, (a, b), axis=1)
    return h


def rglru_scan(x, w_r, b_r, w_i, b_i, lam, h0):
    f = lambda t: t.astype(F32)
    r = jax.nn.sigmoid(blockdiag(x, f(w_r), f(b_r)))
    i = jax.nn.sigmoid(blockdiag(x, f(w_i), f(b_i)))
    log_a = -LRU_C * jax.nn.softplus(-f(lam)) * r
    a = jnp.exp(log_a)
    b = jnp.sqrt(-jnp.expm1(2.0 * log_a)) * (i * x)
    return linear_scan(a, b, h0)


def rglru_branch(parts_c, parts_l, conv_w, conv_b, w_r, b_r, w_i, b_i, lam):
    xc_raw, gate_c = parts_c
    xl_raw, gate_l = parts_l
    xc = dwconv(xc_raw, conv_w, conv_b).astype(F32)
    xl = dwconv(xl_raw, conv_w, conv_b).astype(F32)
    h0 = jnp.zeros((xc.shape[0], C_WIDTH), F32)
    hc = jnp.zeros_like(xc)
    hl = jnp.zeros_like(xl)
    for d in range(2):
        fl = lambda t: flip_if(t, d == 1, 1)
        hcd = rglru_scan(fl(xc), w_r[d], b_r[d], w_i[d], b_i[d], lam[d], h0)
        hld = rglru_scan(fl(xl), w_r[d], b_r[d], w_i[d], b_i[d], lam[d], hcd[:, -1])
        hc = hc + fl(hcd)
        hl = hl + fl(hld)
    yc = (hc * jax.nn.gelu(gate_c.astype(F32))).astype(gate_c.dtype)
    yl = (hl * jax.nn.gelu(gate_l.astype(F32))).astype(gate_l.dtype)
    return yc, yl


def merge_branches(ys, gate_logits, w_branch, w_out):
    gates = jnp.split(jax.nn.sigmoid(gate_logits.astype(F32)), N_BRANCH, axis=-1)
    y = gates[0] * matmul3(ys[0], w_branch[0])
    for k in range(1, N_BRANCH):
        y = y + gates[k] * matmul3(ys[k], w_branch[k])
    return matmul3(y, w_out)


def token_mixer(hc, hl, rope, layer, need_ctx, w_in,
                attn_qn_g, attn_kn_g, lam_q1, lam_k1, lam_q2, lam_k2, attn_sub_g,
                gdn_conv_w, gdn_a_log, gdn_dt_bias, gdn_norm_g,
                lru_conv_w, lru_conv_b, lru_w_r, lru_b_r, lru_w_i, lru_b_i, lru_lambda,
                w_branch, w_out):
    offs = np.cumsum((0,) + IN_SPLITS)
    big = [0, 1, 2, 5, 6, 7]
    w_big = jnp.concatenate([w_in[:, offs[i]:offs[i + 1]] for i in big], axis=1).astype(BF16)
    w_small = jnp.pad(w_in[:, offs[3]:offs[5]], ((0, 0), (0, 128 - 4 * B_HEADS))).astype(BF16)
    big_offs = np.cumsum([IN_SPLITS[i] for i in big])[:-1].tolist()

    def project(h):
        pb = jnp.split(matmul3(h, w_big), big_offs, axis=-1)
        ps = matmul3(h, w_small, tn=128)
        return [pb[0], pb[1], pb[2], ps[..., :2 * B_HEADS], ps[..., 2 * B_HEADS:4 * B_HEADS], pb[3], pb[4], pb[5]]

    pc = project(hc)
    pl_ = project(hl)
    ya_c, ya_l = diff_attention(pc[0], pl_[0], rope, layer, need_ctx, attn_qn_g, attn_kn_g,
                                lam_q1, lam_k1, lam_q2, lam_k2, attn_sub_g)
    yb_c, yb_l = gated_deltanet(pc[1:5], pl_[1:5], gdn_conv_w, gdn_a_log, gdn_dt_bias, gdn_norm_g)
    yr_c, yr_l = rglru_branch(pc[5:7], pl_[5:7], lru_conv_w, lru_conv_b, lru_w_r, lru_b_r,
                              lru_w_i, lru_b_i, lru_lambda)
    wb = w_branch.astype(BF16)
    wo = w_out.astype(BF16)
    yl = merge_branches((ya_l, yb_l, yr_l), pl_[7], wb, wo)
    yc = merge_branches((ya_c, yb_c, yr_c), pc[7], wb, wo) if need_ctx else None
    return yc, yl


PEER_TOKENS = 8
PEER_PAIRS = P_HEADS * P_TOPK


PEER_LOOKAHEAD = 4
SUBLANES = 8
PEER_GROUPS = PEER_PAIRS // SUBLANES


def _peer_expert_kernel(idx_cur, idx_nxt, h_ref, gate_ref, uv_hbm, o_ref, buf, sem, *, nsteps):
    i = pl.program_id(0)
    dm = h_ref.shape[1]

    def slot_ref(t):
        return buf.at[pl.ds(t * PEER_GROUPS, PEER_GROUPS)]

    def issue(idx_ref, t):
        def body(g, carry):
            for j in range(SUBLANES):
                pltpu.make_async_copy(uv_hbm.at[idx_ref[t, g * SUBLANES + j]], buf.at[t * PEER_GROUPS + g, j],
                                      sem.at[t]).start()
            return carry
        lax.fori_loop(0, PEER_GROUPS, body, 0)

    @pl.when(i == 0)
    def _():
        for t in range(PEER_LOOKAHEAD):
            issue(idx_cur, t)

    hb = h_ref[...].astype(BF16)
    gate = gate_ref[...]
    tok = lax.broadcasted_iota(jnp.int32, gate.shape, 0)
    acc = jnp.zeros(o_ref.shape, F32)
    for t in range(PEER_TOKENS):
        ahead = t + PEER_LOOKAHEAD
        if ahead < PEER_TOKENS:
            issue(idx_cur, ahead)
        else:
            pl.when(i + 1 < nsteps)(functools.partial(issue, idx_nxt, ahead - PEER_TOKENS))
        pltpu.make_async_copy(slot_ref(t), slot_ref(t), sem.at[t]).wait()
        u_t = buf[t * PEER_GROUPS:(t + 1) * PEER_GROUPS, :, :dm].reshape(PEER_PAIRS, dm).astype(BF16)
        v_t = buf[t * PEER_GROUPS:(t + 1) * PEER_GROUPS, :, dm:].reshape(PEER_PAIRS, dm).astype(BF16)
        s = lax.dot_general(hb, u_t, (((1,), (1,)), ((), ())), preferred_element_type=F32)
        a = jnp.where(tok == t, jax.nn.gelu(s) * gate, 0.0)
        acc = acc + jnp.dot(a.astype(BF16), v_t, preferred_element_type=F32)
    o_ref[...] = acc


def peer_experts(h, idx, gate, uv):
    ntok, dm = h.shape
    assert ntok % PEER_TOKENS == 0 and idx.shape == (ntok, PEER_PAIRS)
    nsteps = ntok // PEER_TOKENS
    tok_block = lambda i: (i, 0)
    nxt_block = lambda i: (jnp.minimum(i + 1, nsteps - 1), 0)
    return pl.pallas_call(
        functools.partial(_peer_expert_kernel, nsteps=nsteps),
        grid=(nsteps,),
        in_specs=[pl.BlockSpec((PEER_TOKENS, PEER_PAIRS), tok_block, memory_space=pltpu.SMEM),
                  pl.BlockSpec((PEER_TOKENS, PEER_PAIRS), nxt_block, memory_space=pltpu.SMEM),
                  pl.BlockSpec((PEER_TOKENS, dm), tok_block),
                  pl.BlockSpec((PEER_TOKENS, PEER_PAIRS), tok_block),
                  pl.BlockSpec(memory_space=pl.ANY)],
        out_specs=pl.BlockSpec((PEER_TOKENS, dm), tok_block),
        out_shape=jax.ShapeDtypeStruct((ntok, dm), F32),
        scratch_shapes=[pltpu.VMEM((PEER_TOKENS * PEER_GROUPS, SUBLANES, 2 * dm), F32),
                        pltpu.SemaphoreType.DMA((PEER_TOKENS,))],
        compiler_params=pltpu.CompilerParams(dimension_semantics=("arbitrary",)),
        name="peer_experts",
    )(idx, idx, h, gate, uv)


def peer_ffn(h, wq, subkeys, uv):
    ntok, dm = h.shape
    q = rmsnorm(matmul(h, wq.astype(BF16)).reshape(ntok, P_HEADS, 2, P_QDIM // 2))
    s = jnp.einsum('nhcd,hckd->nhck', q, subkeys).astype(F32)
    s_top, i_top = lax.top_k(s, P_TOPK)
    cand = (s_top[..., 0, :, None] + s_top[..., 1, None, :]).reshape(ntok, P_HEADS, P_TOPK * P_TOPK)
    cand_idx = (i_top[..., 0, :, None] * P_KEYS + i_top[..., 1, None, :]).reshape(ntok, P_HEADS, P_TOPK * P_TOPK)
    best, pos = lax.top_k(cand, P_TOPK)
    idx = jnp.take_along_axis(cand_idx, pos, axis=-1)
    gate = jax.nn.softmax(best, axis=-1).astype(h.dtype)
    return peer_experts(h, idx.reshape(ntok, PEER_PAIRS).astype(jnp.int32), gate.reshape(ntok, PEER_PAIRS), uv)


def kernel(x, c, ctx, c_ctx, w_ada, b_ada, norm1_g, norm2_g, w_in, attn_qn_g, attn_kn_g, lam_q1, lam_k1, lam_q2, lam_k2, attn_sub_g, gdn_conv_w, gdn_a_log, gdn_dt_bias, gdn_norm_g, lru_conv_w, lru_conv_b, lru_w_r, lru_b_r, lru_w_i, lru_b_i, lru_lambda, w_branch, w_out, peer_wq, peer_subkeys, peer_u, peer_v):
    n_lat = x.shape[1]
    rope = axial_rope_tables(n_lat // GRID_W, x.dtype)
    dm = x.shape[-1]
    xl, xc = x, ctx
    for l in range(DEPTH):
        need_ctx = l < DEPTH - 1
        w_ada_l = w_ada[l].astype(BF16)
        ml = adaln(c, w_ada_l, b_ada[l])
        mc = adaln(c_ctx[None, :], w_ada_l, b_ada[l])
        hl = modulate(rmsnorm(xl, norm1_g[l]), ml[0], ml[1])
        hc = modulate(rmsnorm(xc, norm1_g[l]), mc[0], mc[1])
        yc, yl = token_mixer(hc, hl, rope, l, need_ctx, w_in[l],
                             attn_qn_g[l], attn_kn_g[l], lam_q1[l], lam_k1[l], lam_q2[l], lam_k2[l], attn_sub_g[l],
                             gdn_conv_w[l], gdn_a_log[l], gdn_dt_bias[l], gdn_norm_g[l],
                             lru_conv_w[l], lru_conv_b[l], lru_w_r[l], lru_b_r[l], lru_w_i[l], lru_b_i[l], lru_lambda[l],
                             w_branch[l], w_out[l])
        xl = xl + ml[2] * yl
        hl = modulate(rmsnorm(xl, norm2_g[l]), ml[3], ml[4]).reshape(-1, dm)
        uv = jnp.concatenate([peer_u[l], peer_v[l]], axis=1)
        if need_ctx:
            xc = xc + mc[2] * yc
            hc = modulate(rmsnorm(xc, norm2_g[l]), mc[3], mc[4]).reshape(-1, dm)
            f = peer_ffn(jnp.concatenate([hc, hl], axis=0), peer_wq[l], peer_subkeys[l], uv)
            xc = xc + mc[5] * f[:hc.shape[0]].reshape(xc.shape)
            xl = xl + ml[5] * f[hc.shape[0]:].reshape(xl.shape)
        else:
            xl = xl + ml[5] * peer_ffn(hl, peer_wq[l], peer_subkeys[l], uv).reshape(xl.shape)
    return xl
```

```python
import functools
import math

import jax
import jax.numpy as jnp
import numpy as np
from jax import lax
from jax.experimental import pallas as pl
from jax.experimental.pallas import tpu as pltpu

D_MODEL = 1024
DEPTH = 4
GRID_W = 64
EPS = 1e-6
N_MOD = 6
A_HEADS = 8
A_DIM = 64
ROPE_THETA = 10000.0
B_HEADS = 8
B_DIM = 128
B_CHUNK = 64
SHORT_CONV = 5
C_WIDTH = 1024
C_BLOCKS = 16
LRU_C = 8.0
N_BRANCH = 3
P_HEADS = 8
P_KEYS = 128
P_QDIM = 256
P_TOPK = 16

A_QKV = 3 * A_HEADS * 2 * A_DIM
B_QKV = 3 * B_HEADS * B_DIM
IN_SPLITS = (A_QKV, B_QKV, B_HEADS * B_DIM, 2 * B_HEADS, 2 * B_HEADS, C_WIDTH, C_WIDTH, N_BRANCH * D_MODEL)

F32 = jnp.float32
BF16 = jnp.bfloat16
HIGHEST = lax.Precision.HIGHEST
NT_DIMS = (((1,), (1,)), ((), ()))
SUBLANES = 8
LANES = 128


def _matmul_kernel(x_ref, w_ref, o_ref):
    o_ref[...] = jnp.dot(x_ref[...].astype(BF16), w_ref[...].astype(BF16),
                         preferred_element_type=F32).astype(o_ref.dtype)


def matmul(x, w, *, tm=1024, tn=512, out_dtype=F32):
    m, k = x.shape
    n = w.shape[1]
    tm, tn = min(tm, m), min(tn, n)
    assert m % tm == 0 and n % tn == 0, (m, n, tm, tn)
    return pl.pallas_call(
        _matmul_kernel,
        grid=(m // tm, n // tn),
        in_specs=[pl.BlockSpec((tm, k), lambda i, j: (i, 0)),
                  pl.BlockSpec((k, tn), lambda i, j: (0, j))],
        out_specs=pl.BlockSpec((tm, tn), lambda i, j: (i, j)),
        out_shape=jax.ShapeDtypeStruct((m, n), out_dtype),
        compiler_params=pltpu.CompilerParams(dimension_semantics=("parallel", "parallel")),
        name="matmul",
    )(x, w)


def matmul3(x, w, **kw):
    b, t, k = x.shape
    return matmul(x.reshape(b * t, k), w, **kw).reshape(b, t, -1)


def _diff_attn_kernel(lam_ref, q_ref, kt_ref, v_ref, g_ref, o_ref, *, out_scale):
    q = q_ref[0]
    kt = kt_ref[0]
    v = v_ref[0]
    lane = lax.broadcasted_iota(jnp.int32, q.shape, 1)
    zero = jnp.zeros_like(q)

    def softmax_v(qc):
        s = jnp.dot(qc, kt, preferred_element_type=F32)
        p = jnp.exp(s - jnp.max(s, axis=-1, keepdims=True))
        l = jnp.sum(p, axis=-1, keepdims=True)
        return jnp.dot(p.astype(BF16), v, preferred_element_type=F32) / l

    o = softmax_v(jnp.where(lane < A_DIM, q, zero)) - lam_ref[0] * softmax_v(jnp.where(lane >= A_DIM, q, zero))
    y = o * lax.rsqrt(jnp.mean(o * o, axis=-1, keepdims=True) + EPS) * g_ref[...]
    o_ref[0] = y * out_scale


def diff_attention_core(q, kt, v, lam, sub_g, out_scale, *, tq=256):
    bn, tq_all, _ = q.shape
    tk = v.shape[1]
    tq = min(tq, tq_all)
    hd = 2 * A_DIM
    return pl.pallas_call(
        functools.partial(_diff_attn_kernel, out_scale=out_scale),
        grid=(bn, A_HEADS, tq_all // tq),
        in_specs=[pl.BlockSpec(memory_space=pltpu.SMEM),
                  pl.BlockSpec((1, tq, hd), lambda b, h, i: (b, i, h)),
                  pl.BlockSpec((1, hd, tk), lambda b, h, i: (b, h, 0)),
                  pl.BlockSpec((1, tk, hd), lambda b, h, i: (b, 0, h)),
                  pl.BlockSpec((1, hd), lambda b, h, i: (0, 0))],
        out_specs=pl.BlockSpec((1, tq, hd), lambda b, h, i: (b, i, h)),
        out_shape=jax.ShapeDtypeStruct((bn, tq_all, A_HEADS * hd), F32),
        compiler_params=pltpu.CompilerParams(dimension_semantics=("parallel", "parallel", "parallel")),
        name="diff_attn",
    )(lam.reshape(1).astype(F32), q, kt, v, sub_g.reshape(1, hd).astype(F32))


BMM = (((2,), (1,)), ((0,), (0,)))
BMM_NT = (((2,), (2,)), ((0,), (0,)))
BMM_TN = (((1,), (1,)), ((0,), (0,)))


def _bdot(a, b, dims=BMM):
    return lax.dot_general(a, b, dims, preferred_element_type=F32)


def _bdot3(a, b):
    a_hi, b_hi = a.astype(BF16), b.astype(BF16)
    a_lo = (a - a_hi.astype(F32)).astype(BF16)
    b_lo = (b - b_hi.astype(F32)).astype(BF16)
    return _bdot(a_hi, b_hi) + (_bdot(a_hi, b_lo) + _bdot(a_lo, b_hi))


def _gdn_kernel(q_ref, k_ref, v_ref, gcol_ref, bcol_ref, grow_ref, o_ref, s_ref, *, rev):
    @pl.when(pl.program_id(1) == 0)
    def _():
        s_ref[...] = jnp.zeros_like(s_ref)

    c = B_CHUNK
    ii = lax.broadcasted_iota(jnp.int32, (c, c), 0)
    jj = lax.broadcasted_iota(jnp.int32, (c, c), 1)
    incl = (ii <= jj) if rev else (ii >= jj)
    strict = (ii < jj) if rev else (ii > jj)
    eye = (ii == jj).astype(F32)
    m_incl = incl.astype(F32)
    last = 0 if rev else c - 1
    hs = range(B_HEADS)

    gc_cols = jnp.dot(m_incl, gcol_ref[0], precision=HIGHEST, preferred_element_type=F32)
    gc_rows = lax.dot_general(grow_ref[0, 0], m_incl, NT_DIMS, precision=HIGHEST, preferred_element_type=F32)
    beta_cols = bcol_ref[0]
    gi = jnp.stack([gc_cols[:, h:h + 1] for h in hs])
    gj = jnp.stack([gc_rows[h:h + 1, :] for h in hs])
    beta = jnp.stack([beta_cols[:, h:h + 1] for h in hs])
    heads = lambda ref: jnp.stack([ref[0, :, h * B_DIM:(h + 1) * B_DIM] for h in hs])
    q, k, v = heads(q_ref), heads(k_ref), heads(v_ref)

    dec = jnp.where(incl, jnp.exp(jnp.where(incl, gi - gj, 0.0)), 0.0)
    kb = k * beta
    kk = _bdot(kb, k, BMM_NT)
    qk = _bdot(q, k, BMM_NT) * dec
    lmat = jnp.where(strict, kk * dec, 0.0)
    same = lambda b: (ii // b) == (jj // b)
    base = 4
    nl = -jnp.where(same(base), lmat, 0.0)
    inv = eye + nl
    inv = inv + _bdot3(inv, _bdot3(nl, nl))
    b = base
    while b < c:
        join = jnp.where(same(2 * b) & jnp.logical_not(same(b)), lmat, 0.0)
        inv = inv - _bdot3(_bdot3(inv, join), inv)
        b *= 2
    eg = jnp.exp(gi)
    uw = _bdot3(inv, jnp.concatenate([v * beta, kb * eg], axis=2))
    u, w = uw[:, :, :B_DIM], uw[:, :, B_DIM:]
    s = s_ref[...]
    v_new = u - _bdot(w, s)
    o = _bdot(q * eg, s) + _bdot(qk, v_new)
    for h in hs:
        o_ref[0, :, h * B_DIM:(h + 1) * B_DIM] = o[h]
    g_last = gi[:, last:last + 1, :]
    kd = k * jnp.exp(g_last - gi)
    s_ref[...] = s * jnp.exp(g_last) + _bdot(kd, v_new, BMM_TN)


def gdn_scan(q, k, v, g, beta, n_ctx, *, rev):
    bn, tn, _ = q.shape
    nc = tn // B_CHUNK
    ncc = n_ctx // B_CHUNK
    g_rows = jnp.swapaxes(g.reshape(bn, nc, B_CHUNK, B_HEADS), 2, 3)

    def chunk(n):
        return jnp.where(n < ncc, ncc - 1 - n, nc - 1 + ncc - n) if rev else n

    tok = lambda b, n: (b, chunk(n), 0)
    wide = pl.BlockSpec((1, B_CHUNK, B_HEADS * B_DIM), tok)
    narrow = pl.BlockSpec((1, B_CHUNK, B_HEADS), tok)
    return pl.pallas_call(
        functools.partial(_gdn_kernel, rev=rev),
        grid=(bn, nc),
        in_specs=[wide, wide, wide, narrow, narrow,
                  pl.BlockSpec((1, 1, B_HEADS, B_CHUNK), lambda b, n: (b, chunk(n), 0, 0))],
        out_specs=wide,
        out_shape=jax.ShapeDtypeStruct((bn, tn, B_HEADS * B_DIM), F32),
        scratch_shapes=[pltpu.VMEM((B_HEADS, B_DIM, B_DIM), F32)],
        compiler_params=pltpu.CompilerParams(dimension_semantics=("parallel", "arbitrary")),
        name="gdn_scan",
    )(q, k, v, g, beta, g_rows)


LRU_CG = 256
LRU_ROWS = 256


def _lru_tile_scan(a, b, h, row, rev):
    for s in (1, 2, 4):
        shift = SUBLANES - s if rev else s
        valid = (row < SUBLANES - s) if rev else (row >= s)
        a_s = pltpu.roll(a, shift, 0)
        b_s = pltpu.roll(b, shift, 0)
        b = jnp.where(valid, a * b_s + b, b)
        a = jnp.where(valid, a * a_s, a)
    ht = a * h + b
    return ht, (ht[0:1] if rev else ht[SUBLANES - 1:SUBLANES])


def _lru_kernel(x_ref, gate_ref, wr_ref, wi_ref, br_ref, bi_ref, c_ref, o_ref, *, n_ctx):
    tn = x_ref.shape[1]
    row = lax.broadcasted_iota(jnp.int32, (SUBLANES, LRU_CG), 0)
    ntile = LRU_ROWS // SUBLANES

    def chunk(r0, h, d, rev):
        xs = x_ref[0, pl.ds(r0, LRU_ROWS), :]
        xb = xs.astype(BF16)
        r = jax.nn.sigmoid(jnp.dot(xb, wr_ref[d, 0], preferred_element_type=F32) + br_ref[d])
        i = jax.nn.sigmoid(jnp.dot(xb, wi_ref[d, 0], preferred_element_type=F32) + bi_ref[d])
        log_a = c_ref[d] * r
        a = jnp.exp(log_a)
        b = jnp.sqrt(jnp.maximum(1.0 - a * a, 0.0)) * (i * xs)
        tiles = [None] * ntile
        for ti in (reversed(range(ntile)) if rev else range(ntile)):
            sl = slice(ti * SUBLANES, (ti + 1) * SUBLANES)
            tiles[ti], h = _lru_tile_scan(a[sl], b[sl], h, row, rev)
        hs = jnp.concatenate(tiles, axis=0)
        if d == 0:
            o_ref[0, pl.ds(r0, LRU_ROWS), :] = hs
        else:
            gate = gate_ref[0, pl.ds(r0, LRU_ROWS), :]
            o_ref[0, pl.ds(r0, LRU_ROWS), :] = (o_ref[0, pl.ds(r0, LRU_ROWS), :] + hs) * jax.nn.gelu(gate)
        return h

    for d, rev in ((0, False), (1, True)):
        h = jnp.zeros((1, LRU_CG), F32)
        for s0, s1 in ((0, n_ctx), (n_ctx, tn)):
            nch = (s1 - s0) // LRU_ROWS

            def body(ci, h, s0=s0, nch=nch, d=d, rev=rev):
                cidx = nch - 1 - ci if rev else ci
                return chunk(pl.multiple_of(s0 + cidx * LRU_ROWS, LRU_ROWS), h, d, rev)

            h = lax.fori_loop(0, nch, body, h)


def lru_scan(x, gate, w_r, b_r, w_i, b_i, lam, n_ctx):
    bn, tn, cw = x.shape
    assert n_ctx % LRU_ROWS == 0 and (tn - n_ctx) % LRU_ROWS == 0 and cw % LRU_CG == 0
    ncg = cw // LRU_CG
    per = LRU_CG // (C_WIDTH // C_BLOCKS)

    def blockdiag_tiles(w):
        blk = w.shape[-1]
        wt = w.astype(F32).reshape(2, ncg, per, blk, blk)
        eye = jnp.eye(per, dtype=F32)
        return jnp.einsum('dgpab,pq->dgpaqb', wt, eye).reshape(2, ncg, LRU_CG, LRU_CG).astype(BF16)

    cvec = (-LRU_C * jax.nn.softplus(-lam.astype(F32))).reshape(2, 1, cw)
    tok = pl.BlockSpec((1, tn, LRU_CG), lambda b, g: (b, 0, g))
    wspec = pl.BlockSpec((2, 1, LRU_CG, LRU_CG), lambda b, g: (0, g, 0, 0))
    vspec = pl.BlockSpec((2, 1, LRU_CG), lambda b, g: (0, 0, g))
    return pl.pallas_call(
        functools.partial(_lru_kernel, n_ctx=n_ctx),
        grid=(bn, ncg),
        in_specs=[tok, tok, wspec, wspec, vspec, vspec, vspec],
        out_specs=tok,
        out_shape=jax.ShapeDtypeStruct((bn, tn, cw), F32),
        compiler_params=pltpu.CompilerParams(dimension_semantics=("parallel", "parallel")),
        name="lru_scan",
    )(x, gate, blockdiag_tiles(w_r), blockdiag_tiles(w_i),
      b_r.astype(F32).reshape(2, 1, cw), b_i.astype(F32).reshape(2, 1, cw), cvec)


ROUTE_TOKENS = 128
PEER_PAIRS = P_HEADS * P_TOPK
N_CAND = P_TOPK * P_TOPK


def _lane_topk(s, k, payload=None):
    g, tb, width = s.shape
    lane = lax.broadcasted_iota(jnp.int32, s.shape, 2).astype(F32)
    out_lane = lax.broadcasted_iota(jnp.int32, (g, tb, LANES), 2)
    vals = jnp.zeros((g, tb, LANES), F32)
    picked = jnp.zeros((g, tb, LANES), F32)
    for r in range(k):
        m = jnp.max(s, axis=-1, keepdims=True)
        p = jnp.min(jnp.where(s == m, lane, float(width)), axis=-1, keepdims=True)
        sel = lane == p
        if payload is not None:
            p = jnp.sum(jnp.where(sel, payload, 0.0), axis=-1, keepdims=True)
        vals = jnp.where(out_lane == r, m, vals)
        picked = jnp.where(out_lane == r, p, picked)
        s = jnp.where(sel, -jnp.inf, s)
    return vals, picked


def _peer_route_kernel(q_ref, sk_ref, idx_ref, gate_ref):
    tb = q_ref.shape[0]
    half = P_QDIM // 2
    er = lax.broadcasted_iota(jnp.int32, (LANES, N_CAND), 0)
    ec = lax.broadcasted_iota(jnp.int32, (LANES, N_CAND), 1)
    e_first = (ec // P_TOPK == er).astype(F32)
    e_second = ((ec % P_TOPK == er) & (er < P_TOPK)).astype(F32)

    def spread(t, e):
        y = jnp.dot(t.reshape(P_HEADS * tb, LANES), e, precision=HIGHEST, preferred_element_type=F32)
        return y.reshape(P_HEADS, tb, N_CAND)

    def scores(c):
        out = []
        for h in range(P_HEADS):
            x = q_ref[:, (2 * h + c) * half:(2 * h + c + 1) * half]
            xn = x * lax.rsqrt(jnp.mean(x * x, axis=-1, keepdims=True) + EPS)
            out.append(lax.dot_general(xn, sk_ref[h, c], NT_DIMS, preferred_element_type=F32))
        return jnp.stack(out)

    v1, i1 = _lane_topk(scores(0), P_TOPK)
    v2, i2 = _lane_topk(scores(1), P_TOPK)
    cand = spread(v1, e_first) + spread(v2, e_second)
    cand_idx = spread(i1, e_first) * float(P_KEYS) + spread(i2, e_second)
    best, eid = _lane_topk(cand, P_TOPK, payload=cand_idx)
    out_lane = lax.broadcasted_iota(jnp.int32, best.shape, 2)
    top = jnp.max(jnp.where(out_lane < P_TOPK, best, -jnp.inf), axis=-1, keepdims=True)
    ex = jnp.where(out_lane < P_TOPK, jnp.exp(best - top), 0.0)
    gate = ex / jnp.sum(ex, axis=-1, keepdims=True)
    place = lambda t, h: pltpu.roll(t[h], h * P_TOPK, 1) if h else t[0]
    idx_ref[...] = sum(place(eid, h) for h in range(P_HEADS)).astype(jnp.int32)
    gate_ref[...] = sum(place(gate, h) for h in range(P_HEADS))


def peer_route(q, subkeys):
    ntok = q.shape[0]
    assert ntok % ROUTE_TOKENS == 0 and PEER_PAIRS == LANES and P_KEYS == LANES
    tok = lambda i: (i, 0)
    return pl.pallas_call(
        _peer_route_kernel,
        grid=(ntok // ROUTE_TOKENS,),
        in_specs=[pl.BlockSpec((ROUTE_TOKENS, q.shape[1]), tok),
                  pl.BlockSpec(subkeys.shape, lambda i: (0, 0, 0, 0))],
        out_specs=[pl.BlockSpec((ROUTE_TOKENS, PEER_PAIRS), tok), pl.BlockSpec((ROUTE_TOKENS, PEER_PAIRS), tok)],
        out_shape=[jax.ShapeDtypeStruct((ntok, PEER_PAIRS), jnp.int32),
                   jax.ShapeDtypeStruct((ntok, PEER_PAIRS), F32)],
        compiler_params=pltpu.CompilerParams(dimension_semantics=("parallel",)),
        name="peer_route",
    )(q, subkeys.astype(F32))


PEER_TOKENS = 8
PEER_LOOKAHEAD = 4
PEER_GROUPS = PEER_PAIRS // SUBLANES


def _peer_expert_kernel(idx_cur, idx_nxt, h_ref, gate_ref, uv_hbm, o_ref, buf, sem, *, nsteps):
    i = pl.program_id(0)
    dm = h_ref.shape[1]

    def slot_ref(t):
        return buf.at[pl.ds(t * PEER_GROUPS, PEER_GROUPS)]

    def issue(idx_ref, t):
        def body(g, carry):
            for j in range(SUBLANES):
                pltpu.make_async_copy(uv_hbm.at[idx_ref[t, g * SUBLANES + j]], buf.at[t * PEER_GROUPS + g, j],
                                      sem.at[t]).start()
            return carry
        lax.fori_loop(0, PEER_GROUPS, body, 0)

    @pl.when(i == 0)
    def _():
        for t in range(PEER_LOOKAHEAD):
            issue(idx_cur, t)

    hb = h_ref[...].astype(BF16)
    gate = gate_ref[...]
    tok = lax.broadcasted_iota(jnp.int32, gate.shape, 0)
    acc = jnp.zeros(o_ref.shape, F32)
    for t in range(PEER_TOKENS):
        ahead = t + PEER_LOOKAHEAD
        if ahead < PEER_TOKENS:
            issue(idx_cur, ahead)
        else:
            pl.when(i + 1 < nsteps)(functools.partial(issue, idx_nxt, ahead - PEER_TOKENS))
        pltpu.make_async_copy(slot_ref(t), slot_ref(t), sem.at[t]).wait()
        u_t = buf[t * PEER_GROUPS:(t + 1) * PEER_GROUPS, :, :dm].reshape(PEER_PAIRS, dm).astype(BF16)
        v_t = buf[t * PEER_GROUPS:(t + 1) * PEER_GROUPS, :, dm:].reshape(PEER_PAIRS, dm).astype(BF16)
        s = lax.dot_general(hb, u_t, NT_DIMS, preferred_element_type=F32)
        a = jnp.where(tok == t, jax.nn.gelu(s) * gate, 0.0)
        acc = acc + jnp.dot(a.astype(BF16), v_t, preferred_element_type=F32)
    o_ref[...] = acc


def peer_experts(h, idx, gate, uv):
    ntok, dm = h.shape
    assert ntok % PEER_TOKENS == 0 and idx.shape == (ntok, PEER_PAIRS)
    nsteps = ntok // PEER_TOKENS
    tok_block = lambda i: (i, 0)
    nxt_block = lambda i: (jnp.minimum(i + 1, nsteps - 1), 0)
    return pl.pallas_call(
        functools.partial(_peer_expert_kernel, nsteps=nsteps),
        grid=(nsteps,),
        in_specs=[pl.BlockSpec((PEER_TOKENS, PEER_PAIRS), tok_block, memory_space=pltpu.SMEM),
                  pl.BlockSpec((PEER_TOKENS, PEER_PAIRS), nxt_block, memory_space=pltpu.SMEM),
                  pl.BlockSpec((PEER_TOKENS, dm), tok_block),
                  pl.BlockSpec((PEER_TOKENS, PEER_PAIRS), tok_block),
                  pl.BlockSpec(memory_space=pl.ANY)],
        out_specs=pl.BlockSpec((PEER_TOKENS, dm), tok_block),
        out_shape=jax.ShapeDtypeStruct((ntok, dm), F32),
        scratch_shapes=[pltpu.VMEM((PEER_TOKENS * PEER_GROUPS, SUBLANES, 2 * dm), F32),
                        pltpu.SemaphoreType.DMA((PEER_TOKENS,))],
        compiler_params=pltpu.CompilerParams(dimension_semantics=("arbitrary",)),
        name="peer_experts",
    )(idx, idx, h, gate, uv)


def rmsnorm(x, g=None):
    y = x * lax.rsqrt(jnp.mean(x * x, axis=-1, keepdims=True) + EPS)
    return y if g is None else y * g.astype(F32)


def l2norm(x):
    return x * lax.rsqrt(jnp.sum(x * x, axis=-1, keepdims=True) + EPS)


def adaln(cond, w, b):
    n = cond.shape[0]
    cp = jnp.pad(jax.nn.silu(cond), ((0, (-n) % SUBLANES), (0, 0)))
    m = (matmul(cp, w)[:n] + b)[:, None, :]
    return jnp.split(m, N_MOD, axis=-1)


def dwconv(x, w, b=None):
    pad = SHORT_CONV // 2
    y = lax.conv_general_dilated(x, w[:, None, :].astype(x.dtype), window_strides=(1,),
                                 padding=[(pad, pad)], dimension_numbers=('NWC', 'WIO', 'NWC'),
                                 feature_group_count=x.shape[-1])
    return y if b is None else y + b.astype(x.dtype)


def segment_conv(x, n_ctx, w, b=None):
    return jnp.concatenate([dwconv(x[:, :n_ctx], w, b), dwconv(x[:, n_ctx:], w, b)], axis=1)


def axial_rope_tables(rows, dtype):
    row = jnp.repeat(jnp.arange(rows, dtype=F32), GRID_W)
    col = jnp.tile(jnp.arange(GRID_W, dtype=F32), rows)
    nf = A_DIM // 4
    inv = ROPE_THETA ** (-jnp.arange(nf, dtype=F32) / nf)
    ar, ac = row[:, None] * inv, col[:, None] * inv
    return tuple(t.astype(dtype) for t in (jnp.cos(ar), jnp.sin(ar), jnp.cos(ac), jnp.sin(ac)))


def rope_half(x, cos, sin):
    x1, x2 = jnp.split(x, 2, axis=-1)
    cs, sn = cos[:, None, None, :], sin[:, None, None, :]
    return jnp.concatenate([x1 * cs - x2 * sn, x2 * cs + x1 * sn], axis=-1)


def axial_rope(x, tabs):
    cr, sr, cc, sc = tabs
    xr, xc = jnp.split(x, 2, axis=-1)
    return jnp.concatenate([rope_half(xr, cr, sr), rope_half(xc, cc, sc)], axis=-1)


def diff_attention(qkv, n_ctx, rope, layer, need_ctx, qn_g, kn_g, lq1, lk1, lq2, lk2, sub_g):
    lam_init = 0.8 - 0.6 * math.exp(-0.3 * layer)
    f = lambda t: t.astype(F32)
    lam = jnp.exp(jnp.sum(f(lq1) * f(lk1))) - jnp.exp(jnp.sum(f(lq2) * f(lk2))) + lam_init
    q, k, v = jnp.split(qkv, 3, axis=-1)
    bn, tn, _ = q.shape

    def norm_rotate(t, g):
        t = rmsnorm(t.reshape(bn, tn, A_HEADS, 2, A_DIM), g)
        return jnp.concatenate([t[:, :n_ctx], axial_rope(t[:, n_ctx:], rope)], axis=1).reshape(bn, tn, -1)

    q = (norm_rotate(q, qn_g) * (A_DIM ** -0.5)).astype(BF16)
    kt = jnp.swapaxes(norm_rotate(k, kn_g).astype(BF16), 1, 2)
    v = v.astype(BF16)
    scale = 1.0 - lam_init
    yl = diff_attention_core(q[:, n_ctx:], kt, v, lam, sub_g, scale)
    if need_ctx:
        yc = diff_attention_core(q[:, :n_ctx], kt[:, :, :n_ctx], v[:, :n_ctx], lam, sub_g, scale)
    else:
        yc = jnp.zeros((bn, n_ctx, yl.shape[-1]), F32)
    return jnp.concatenate([yc, yl], axis=1)


def gated_deltanet(qkv, z, b_raw, a_raw, n_ctx, conv_w, a_log, dt_bias, norm_g):
    bn, tn, _ = qkv.shape
    qkv = jax.nn.silu(segment_conv(qkv, n_ctx, conv_w))
    q, k, v = jnp.split(qkv, 3, axis=-1)
    heads = lambda t: t.reshape(bn, tn, B_HEADS, B_DIM)
    q = (l2norm(heads(q)) * (B_DIM ** -0.5)).reshape(bn, tn, -1)
    k = l2norm(heads(k)).reshape(bn, tn, -1)
    beta = jax.nn.sigmoid(b_raw)
    g = -jnp.exp(a_log.astype(F32)).reshape(-1) * jax.nn.softplus(a_raw + dt_bias.astype(F32).reshape(-1))
    o = (gdn_scan(q, k, v, g[..., :B_HEADS], beta[..., :B_HEADS], n_ctx, rev=False)
         + gdn_scan(q, k, v, g[..., B_HEADS:], beta[..., B_HEADS:], n_ctx, rev=True))
    y = rmsnorm(heads(o), norm_g) * jax.nn.silu(heads(z))
    return y.reshape(bn, tn, -1)


def token_mixer(h, n_ctx, rope, layer, need_ctx, w_in,
                attn_qn_g, attn_kn_g, lam_q1, lam_k1, lam_q2, lam_k2, attn_sub_g,
                gdn_conv_w, gdn_a_log, gdn_dt_bias, gdn_norm_g,
                lru_conv_w, lru_conv_b, lru_w_r, lru_b_r, lru_w_i, lru_b_i, lru_lambda,
                w_branch, w_out):
    offs = np.cumsum((0,) + IN_SPLITS)
    big = [0, 1, 2, 5, 6, 7]
    w_big = jnp.concatenate([w_in[:, offs[i]:offs[i + 1]] for i in big], axis=1).astype(BF16)
    w_small = jnp.pad(w_in[:, offs[3]:offs[5]], ((0, 0), (0, LANES - 4 * B_HEADS))).astype(BF16)
    big_offs = np.cumsum([IN_SPLITS[i] for i in big])[:-1].tolist()
    p_attn, p_gdn, p_z, p_lru, p_lru_gate, p_merge = jnp.split(matmul3(h, w_big), big_offs, axis=-1)
    p_small = matmul3(h, w_small, tn=LANES)
    b_raw, a_raw = p_small[..., :2 * B_HEADS], p_small[..., 2 * B_HEADS:4 * B_HEADS]

    ya = diff_attention(p_attn, n_ctx, rope, layer, need_ctx, attn_qn_g, attn_kn_g,
                        lam_q1, lam_k1, lam_q2, lam_k2, attn_sub_g)
    yb = gated_deltanet(p_gdn, p_z, b_raw, a_raw, n_ctx, gdn_conv_w, gdn_a_log, gdn_dt_bias, gdn_norm_g)
    yr = lru_scan(segment_conv(p_lru, n_ctx, lru_conv_w, lru_conv_b), p_lru_gate,
                  lru_w_r, lru_b_r, lru_w_i, lru_b_i, lru_lambda, n_ctx)
    wb = w_branch.astype(BF16)
    gates = jnp.split(jax.nn.sigmoid(p_merge), N_BRANCH, axis=-1)
    y = gates[0] * matmul3(ya, wb[0])
    for j, yj in ((1, yb), (2, yr)):
        y = y + gates[j] * matmul3(yj, wb[j])
    return matmul3(y, w_out.astype(BF16))


def peer_ffn(h, wq, subkeys, uv):
    idx, gate = peer_route(matmul(h, wq.astype(BF16)), subkeys)
    return peer_experts(h, idx, gate, uv)


def kernel(x, c, ctx, c_ctx, w_ada, b_ada, norm1_g, norm2_g, w_in, attn_qn_g, attn_kn_g, lam_q1, lam_k1, lam_q2, lam_k2, attn_sub_g, gdn_conv_w, gdn_a_log, gdn_dt_bias, gdn_norm_g, lru_conv_w, lru_conv_b, lru_w_r, lru_b_r, lru_w_i, lru_b_i, lru_lambda, w_branch, w_out, peer_wq, peer_subkeys, peer_u, peer_v):
    bn, n_lat, dm = x.shape
    n_ctx = ctx.shape[1]
    rope = axial_rope_tables(n_lat // GRID_W, x.dtype)
    is_ctx = (jnp.arange(n_ctx + n_lat) < n_ctx)[None, :, None]
    xa = jnp.concatenate([ctx, x], axis=1)
    for l in range(DEPTH):
        need_ctx = l < DEPTH - 1
        w_ada_l = w_ada[l].astype(BF16)
        ml = adaln(c, w_ada_l, b_ada[l])
        mc = adaln(c_ctx[None, :], w_ada_l, b_ada[l])
        mod = [jnp.where(is_ctx, mc[j], ml[j]) for j in range(N_MOD)]
        h = rmsnorm(xa, norm1_g[l]) * (1.0 + mod[1]) + mod[0]
        y = token_mixer(h, n_ctx, rope, l, need_ctx, w_in[l],
                        attn_qn_g[l], attn_kn_g[l], lam_q1[l], lam_k1[l], lam_q2[l], lam_k2[l], attn_sub_g[l],
                        gdn_conv_w[l], gdn_a_log[l], gdn_dt_bias[l], gdn_norm_g[l],
                        lru_conv_w[l], lru_conv_b[l], lru_w_r[l], lru_b_r[l], lru_w_i[l], lru_b_i[l], lru_lambda[l],
                        w_branch[l], w_out[l])
        xa = xa + mod[2] * y
        if not need_ctx:
            xa, mod = xa[:, n_ctx:], [ml[j] for j in range(N_MOD)]
        h = rmsnorm(xa, norm2_g[l]) * (1.0 + mod[4]) + mod[3]
        uv = jnp.concatenate([peer_u[l], peer_v[l]], axis=1)
        xa = xa + mod[5] * peer_ffn(h.reshape(-1, dm), peer_wq[l], peer_subkeys[l], uv).reshape(xa.shape)
    return xa
```

```python
import functools
import math

import jax
import jax.numpy as jnp
import numpy as np
from jax import lax
from jax.experimental import pallas as pl
from jax.experimental.pallas import tpu as pltpu

D_MODEL = 1024
DEPTH = 4
GRID_W = 64
EPS = 1e-6
N_MOD = 6
A_HEADS = 8
A_DIM = 64
ROPE_THETA = 10000.0
B_HEADS = 8
B_DIM = 128
B_CHUNK = 64
SHORT_CONV = 5
C_WIDTH = 1024
C_BLOCKS = 16
LRU_C = 8.0
N_BRANCH = 3
P_HEADS = 8
P_KEYS = 128
P_QDIM = 256
P_TOPK = 16

A_QKV = 3 * A_HEADS * 2 * A_DIM
B_QKV = 3 * B_HEADS * B_DIM
IN_SPLITS = (A_QKV, B_QKV, B_HEADS * B_DIM, 2 * B_HEADS, 2 * B_HEADS, C_WIDTH, C_WIDTH, N_BRANCH * D_MODEL)

F32 = jnp.float32
BF16 = jnp.bfloat16
HIGHEST = lax.Precision.HIGHEST
NT_DIMS = (((1,), (1,)), ((), ()))
SUBLANES = 8
LANES = 128


def _matmul_kernel(x_ref, w_ref, o_ref):
    o_ref[...] = jnp.dot(x_ref[...].astype(BF16), w_ref[...].astype(BF16),
                         preferred_element_type=F32).astype(o_ref.dtype)


def matmul(x, w, *, tm=1024, tn=512, out_dtype=F32):
    m, k = x.shape
    n = w.shape[1]
    tm, tn = min(tm, m), min(tn, n)
    assert m % tm == 0 and n % tn == 0, (m, n, tm, tn)
    return pl.pallas_call(
        _matmul_kernel,
        grid=(m // tm, n // tn),
        in_specs=[pl.BlockSpec((tm, k), lambda i, j: (i, 0)),
                  pl.BlockSpec((k, tn), lambda i, j: (0, j))],
        out_specs=pl.BlockSpec((tm, tn), lambda i, j: (i, j)),
        out_shape=jax.ShapeDtypeStruct((m, n), out_dtype),
        compiler_params=pltpu.CompilerParams(dimension_semantics=("parallel", "parallel")),
        name="matmul",
    )(x, w)


def matmul3(x, w, **kw):
    b, t, k = x.shape
    return matmul(x.reshape(b * t, k), w, **kw).reshape(b, t, -1)


def _diff_attn_kernel(lam_ref, q_ref, kt_ref, v_ref, g_ref, o_ref, *, out_scale):
    q = q_ref[0]
    kt = kt_ref[0]
    v = v_ref[0]
    lane = lax.broadcasted_iota(jnp.int32, q.shape, 1)
    zero = jnp.zeros_like(q)

    def softmax_v(qc):
        s = jnp.dot(qc, kt, preferred_element_type=F32)
        p = jnp.exp(s - jnp.max(s, axis=-1, keepdims=True))
        l = jnp.sum(p, axis=-1, keepdims=True)
        return jnp.dot(p.astype(BF16), v, preferred_element_type=F32) / l

    o = softmax_v(jnp.where(lane < A_DIM, q, zero)) - lam_ref[0] * softmax_v(jnp.where(lane >= A_DIM, q, zero))
    y = o * lax.rsqrt(jnp.mean(o * o, axis=-1, keepdims=True) + EPS) * g_ref[...]
    o_ref[0] = y * out_scale


def diff_attention_core(q, kt, v, lam, sub_g, out_scale, *, tq=256):
    bn, tq_all, _ = q.shape
    tk = v.shape[1]
    tq = min(tq, tq_all)
    hd = 2 * A_DIM
    return pl.pallas_call(
        functools.partial(_diff_attn_kernel, out_scale=out_scale),
        grid=(bn, A_HEADS, tq_all // tq),
        in_specs=[pl.BlockSpec(memory_space=pltpu.SMEM),
                  pl.BlockSpec((1, tq, hd), lambda b, h, i: (b, i, h)),
                  pl.BlockSpec((1, hd, tk), lambda b, h, i: (b, h, 0)),
                  pl.BlockSpec((1, tk, hd), lambda b, h, i: (b, 0, h)),
                  pl.BlockSpec((1, hd), lambda b, h, i: (0, 0))],
        out_specs=pl.BlockSpec((1, tq, hd), lambda b, h, i: (b, i, h)),
        out_shape=jax.ShapeDtypeStruct((bn, tq_all, A_HEADS * hd), F32),
        compiler_params=pltpu.CompilerParams(dimension_semantics=("parallel", "parallel", "parallel")),
        name="diff_attn",
    )(lam.reshape(1).astype(F32), q, kt, v, sub_g.reshape(1, hd).astype(F32))


BMM = (((2,), (1,)), ((0,), (0,)))
BMM_NT = (((2,), (2,)), ((0,), (0,)))
BMM_TN = (((1,), (1,)), ((0,), (0,)))


def _bdot(a, b, dims=BMM):
    return lax.dot_general(a, b, dims, preferred_element_type=F32)


def _bdot3(a, b):
    a_hi, b_hi = a.astype(BF16), b.astype(BF16)
    a_lo = (a - a_hi.astype(F32)).astype(BF16)
    b_lo = (b - b_hi.astype(F32)).astype(BF16)
    return _bdot(a_hi, b_hi) + (_bdot(a_hi, b_lo) + _bdot(a_lo, b_hi))


def _gdn_kernel(q_ref, k_ref, v_ref, gcol_ref, bcol_ref, grow_ref, o_ref, s_ref, *, rev):
    @pl.when(pl.program_id(1) == 0)
    def _():
        s_ref[...] = jnp.zeros_like(s_ref)

    c = B_CHUNK
    ii = lax.broadcasted_iota(jnp.int32, (c, c), 0)
    jj = lax.broadcasted_iota(jnp.int32, (c, c), 1)
    incl = (ii <= jj) if rev else (ii >= jj)
    strict = (ii < jj) if rev else (ii > jj)
    eye = (ii == jj).astype(F32)
    m_incl = incl.astype(F32)
    last = 0 if rev else c - 1
    hs = range(B_HEADS)

    gc_cols = jnp.dot(m_incl, gcol_ref[0], precision=HIGHEST, preferred_element_type=F32)
    gc_rows = lax.dot_general(grow_ref[0, 0], m_incl, NT_DIMS, precision=HIGHEST, preferred_element_type=F32)
    beta_cols = bcol_ref[0]
    gi = jnp.stack([gc_cols[:, h:h + 1] for h in hs])
    gj = jnp.stack([gc_rows[h:h + 1, :] for h in hs])
    beta = jnp.stack([beta_cols[:, h:h + 1] for h in hs])
    heads = lambda ref: jnp.stack([ref[0, :, h * B_DIM:(h + 1) * B_DIM] for h in hs])
    q, k, v = heads(q_ref), heads(k_ref), heads(v_ref)

    dec = jnp.where(incl, jnp.exp(jnp.where(incl, gi - gj, 0.0)), 0.0)
    kb = k * beta
    kk = _bdot(kb, k, BMM_NT)
    qk = _bdot(q, k, BMM_NT) * dec
    lmat = jnp.where(strict, kk * dec, 0.0)
    same = lambda b: (ii // b) == (jj // b)
    base = 4
    nl = -jnp.where(same(base), lmat, 0.0)
    inv = eye + nl
    inv = inv + _bdot3(inv, _bdot3(nl, nl))
    b = base
    while b < c:
        join = jnp.where(same(2 * b) & jnp.logical_not(same(b)), lmat, 0.0)
        inv = inv - _bdot3(_bdot3(inv, join), inv)
        b *= 2
    eg = jnp.exp(gi)
    uw = _bdot3(inv, jnp.concatenate([v * beta, kb * eg], axis=2))
    u, w = uw[:, :, :B_DIM], uw[:, :, B_DIM:]
    s = s_ref[...]
    v_new = u - _bdot(w, s)
    o = _bdot(q * eg, s) + _bdot(qk, v_new)
    for h in hs:
        o_ref[0, :, h * B_DIM:(h + 1) * B_DIM] = o[h]
    g_last = gi[:, last:last + 1, :]
    kd = k * jnp.exp(g_last - gi)
    s_ref[...] = s * jnp.exp(g_last) + _bdot(kd, v_new, BMM_TN)


def gdn_scan(q, k, v, g, beta, n_ctx, *, rev):
    bn, tn, _ = q.shape
    nc = tn // B_CHUNK
    ncc = n_ctx // B_CHUNK
    g_rows = jnp.swapaxes(g.reshape(bn, nc, B_CHUNK, B_HEADS), 2, 3)

    def chunk(n):
        return jnp.where(n < ncc, ncc - 1 - n, nc - 1 + ncc - n) if rev else n

    tok = lambda b, n: (b, chunk(n), 0)
    wide = pl.BlockSpec((1, B_CHUNK, B_HEADS * B_DIM), tok)
    narrow = pl.BlockSpec((1, B_CHUNK, B_HEADS), tok)
    return pl.pallas_call(
        functools.partial(_gdn_kernel, rev=rev),
        grid=(bn, nc),
        in_specs=[wide, wide, wide, narrow, narrow,
                  pl.BlockSpec((1, 1, B_HEADS, B_CHUNK), lambda b, n: (b, chunk(n), 0, 0))],
        out_specs=wide,
        out_shape=jax.ShapeDtypeStruct((bn, tn, B_HEADS * B_DIM), F32),
        scratch_shapes=[pltpu.VMEM((B_HEADS, B_DIM, B_DIM), F32)],
        compiler_params=pltpu.CompilerParams(dimension_semantics=("parallel", "arbitrary")),
        name="gdn_scan",
    )(q, k, v, g, beta, g_rows)


LRU_CG = 256
LRU_ROWS = 256


def _lru_tile_scan(a, b, h, row, rev):
    for s in (1, 2, 4):
        shift = SUBLANES - s if rev else s
        valid = (row < SUBLANES - s) if rev else (row >= s)
        a_s = pltpu.roll(a, shift, 0)
        b_s = pltpu.roll(b, shift, 0)
        b = jnp.where(valid, a * b_s + b, b)
        a = jnp.where(valid, a * a_s, a)
    ht = a * h + b
    return ht, (ht[0:1] if rev else ht[SUBLANES - 1:SUBLANES])


def _lru_kernel(x_ref, gate_ref, wr_ref, wi_ref, br_ref, bi_ref, c_ref, o_ref, *, n_ctx):
    tn = x_ref.shape[1]
    row = lax.broadcasted_iota(jnp.int32, (SUBLANES, LRU_CG), 0)
    ntile = LRU_ROWS // SUBLANES

    def chunk(r0, h, d, rev):
        xs = x_ref[0, pl.ds(r0, LRU_ROWS), :]
        xb = xs.astype(BF16)
        r = jax.nn.sigmoid(jnp.dot(xb, wr_ref[d, 0], preferred_element_type=F32) + br_ref[d])
        i = jax.nn.sigmoid(jnp.dot(xb, wi_ref[d, 0], preferred_element_type=F32) + bi_ref[d])
        log_a = c_ref[d] * r
        a = jnp.exp(log_a)
        b = jnp.sqrt(jnp.maximum(1.0 - a * a, 0.0)) * (i * xs)
        tiles = [None] * ntile
        for ti in (reversed(range(ntile)) if rev else range(ntile)):
            sl = slice(ti * SUBLANES, (ti + 1) * SUBLANES)
            tiles[ti], h = _lru_tile_scan(a[sl], b[sl], h, row, rev)
        hs = jnp.concatenate(tiles, axis=0)
        if d == 0:
            o_ref[0, pl.ds(r0, LRU_ROWS), :] = hs
        else:
            gate = gate_ref[0, pl.ds(r0, LRU_ROWS), :]
            o_ref[0, pl.ds(r0, LRU_ROWS), :] = (o_ref[0, pl.ds(r0, LRU_ROWS), :] + hs) * jax.nn.gelu(gate)
        return h

    for d, rev in ((0, False), (1, True)):
        h = jnp.zeros((1, LRU_CG), F32)
        for s0, s1 in ((0, n_ctx), (n_ctx, tn)):
            nch = (s1 - s0) // LRU_ROWS

            def body(ci, h, s0=s0, nch=nch, d=d, rev=rev):
                cidx = nch - 1 - ci if rev else ci
                return chunk(pl.multiple_of(s0 + cidx * LRU_ROWS, LRU_ROWS), h, d, rev)

            h = lax.fori_loop(0, nch, body, h)


def lru_scan(x, gate, w_r, b_r, w_i, b_i, lam, n_ctx):
    bn, tn, cw = x.shape
    assert n_ctx % LRU_ROWS == 0 and (tn - n_ctx) % LRU_ROWS == 0 and cw % LRU_CG == 0
    ncg = cw // LRU_CG
    per = LRU_CG // (C_WIDTH // C_BLOCKS)

    def blockdiag_tiles(w):
        blk = w.shape[-1]
        wt = w.astype(F32).reshape(2, ncg, per, blk, blk)
        eye = jnp.eye(per, dtype=F32)
        return jnp.einsum('dgpab,pq->dgpaqb', wt, eye).reshape(2, ncg, LRU_CG, LRU_CG).astype(BF16)

    cvec = (-LRU_C * jax.nn.softplus(-lam.astype(F32))).reshape(2, 1, cw)
    tok = pl.BlockSpec((1, tn, LRU_CG), lambda b, g: (b, 0, g))
    wspec = pl.BlockSpec((2, 1, LRU_CG, LRU_CG), lambda b, g: (0, g, 0, 0))
    vspec = pl.BlockSpec((2, 1, LRU_CG), lambda b, g: (0, 0, g))
    return pl.pallas_call(
        functools.partial(_lru_kernel, n_ctx=n_ctx),
        grid=(bn, ncg),
        in_specs=[tok, tok, wspec, wspec, vspec, vspec, vspec],
        out_specs=tok,
        out_shape=jax.ShapeDtypeStruct((bn, tn, cw), F32),
        compiler_params=pltpu.CompilerParams(dimension_semantics=("parallel", "parallel")),
        name="lru_scan",
    )(x, gate, blockdiag_tiles(w_r), blockdiag_tiles(w_i),
      b_r.astype(F32).reshape(2, 1, cw), b_i.astype(F32).reshape(2, 1, cw), cvec)


ROUTE_TOKENS = 128
PEER_PAIRS = P_HEADS * P_TOPK
N_CAND = P_TOPK * P_TOPK


def _lane_topk(s, k, payload=None):
    g, tb, width = s.shape
    lane = lax.broadcasted_iota(jnp.int32, s.shape, 2).astype(F32)
    out_lane = lax.broadcasted_iota(jnp.int32, (g, tb, LANES), 2)
    vals = jnp.zeros((g, tb, LANES), F32)
    picked = jnp.zeros((g, tb, LANES), F32)
    for r in range(k):
        m = jnp.max(s, axis=-1, keepdims=True)
        p = jnp.min(jnp.where(s == m, lane, float(width)), axis=-1, keepdims=True)
        sel = lane == p
        if payload is not None:
            p = jnp.sum(jnp.where(sel, payload, 0.0), axis=-1, keepdims=True)
        vals = jnp.where(out_lane == r, m, vals)
        picked = jnp.where(out_lane == r, p, picked)
        s = jnp.where(sel, -jnp.inf, s)
    return vals, picked


def _peer_route_kernel(q_ref, sk_ref, idx_ref, gate_ref):
    tb = q_ref.shape[0]
    half = P_QDIM // 2
    er = lax.broadcasted_iota(jnp.int32, (LANES, N_CAND), 0)
    ec = lax.broadcasted_iota(jnp.int32, (LANES, N_CAND), 1)
    e_first = (ec // P_TOPK == er).astype(F32)
    e_second = ((ec % P_TOPK == er) & (er < P_TOPK)).astype(F32)

    def spread(t, e):
        y = jnp.dot(t.reshape(P_HEADS * tb, LANES), e, precision=HIGHEST, preferred_element_type=F32)
        return y.reshape(P_HEADS, tb, N_CAND)

    def scores(c):
        out = []
        for h in range(P_HEADS):
            x = q_ref[:, (2 * h + c) * half:(2 * h + c + 1) * half]
            xn = x * lax.rsqrt(jnp.mean(x * x, axis=-1, keepdims=True) + EPS)
            out.append(lax.dot_general(xn, sk_ref[h, c], NT_DIMS, preferred_element_type=F32))
        return jnp.stack(out)

    v1, i1 = _lane_topk(scores(0), P_TOPK)
    v2, i2 = _lane_topk(scores(1), P_TOPK)
    cand = spread(v1, e_first) + spread(v2, e_second)
    cand_idx = spread(i1, e_first) * float(P_KEYS) + spread(i2, e_second)
    best, eid = _lane_topk(cand, P_TOPK, payload=cand_idx)
    out_lane = lax.broadcasted_iota(jnp.int32, best.shape, 2)
    top = jnp.max(jnp.where(out_lane < P_TOPK, best, -jnp.inf), axis=-1, keepdims=True)
    ex = jnp.where(out_lane < P_TOPK, jnp.exp(best - top), 0.0)
    gate = ex / jnp.sum(ex, axis=-1, keepdims=True)
    place = lambda t, h: pltpu.roll(t[h], h * P_TOPK, 1) if h else t[0]
    idx_ref[...] = sum(place(eid, h) for h in range(P_HEADS)).astype(jnp.int32)
    gate_ref[...] = sum(place(gate, h) for h in range(P_HEADS))


def peer_route(q, subkeys):
    ntok = q.shape[0]
    assert ntok % ROUTE_TOKENS == 0 and PEER_PAIRS == LANES and P_KEYS == LANES
    tok = lambda i: (i, 0)
    return pl.pallas_call(
        _peer_route_kernel,
        grid=(ntok // ROUTE_TOKENS,),
        in_specs=[pl.BlockSpec((ROUTE_TOKENS, q.shape[1]), tok),
                  pl.BlockSpec(subkeys.shape, lambda i: (0, 0, 0, 0))],
        out_specs=[pl.BlockSpec((ROUTE_TOKENS, PEER_PAIRS), tok), pl.BlockSpec((ROUTE_TOKENS, PEER_PAIRS), tok)],
        out_shape=[jax.ShapeDtypeStruct((ntok, PEER_PAIRS), jnp.int32),
                   jax.ShapeDtypeStruct((ntok, PEER_PAIRS), F32)],
        compiler_params=pltpu.CompilerParams(dimension_semantics=("parallel",)),
        name="peer_route",
    )(q, subkeys.astype(F32))


PEER_TOKENS = 8
PEER_LOOKAHEAD = 4


def _peer_expert_kernel(idx_cur, idx_nxt, h_ref, gate_ref, uv_hbm, o_ref, *scratch, nsteps):
    bufs, sem = scratch[:PEER_TOKENS], scratch[PEER_TOKENS]
    i = pl.program_id(0)
    slab, w = uv_hbm.shape[1:]
    half = slab // 2

    def issue(idx_ref, t, g, j):
        pltpu.make_async_copy(uv_hbm.at[idx_ref[t * PEER_PAIRS + g * SUBLANES + j]], bufs[t].at[:, g, j],
                              sem.at[t]).start()

    def wait(t):
        pltpu.make_async_copy(bufs[t], bufs[t], sem.at[t]).wait()

    @pl.when(i == 0)
    def _():
        for t in range(PEER_LOOKAHEAD):
            def body(g, carry, t=t):
                for j in range(SUBLANES):
                    issue(idx_cur, t, g, j)
                return carry
            lax.fori_loop(0, PEER_PAIRS // SUBLANES, body, 0)

    eye_t = (lax.broadcasted_iota(jnp.int32, (PEER_TOKENS, PEER_TOKENS), 0)
             == lax.broadcasted_iota(jnp.int32, (PEER_TOKENS, PEER_TOKENS), 1))
    tn_dims = (((0,), (0,)), ((), ()))
    hb_t = lax.dot_general(h_ref[...].astype(BF16), eye_t.astype(BF16), tn_dims,
                           preferred_element_type=F32).astype(BF16)
    gate_t = lax.dot_general(gate_ref[...], eye_t.astype(F32), tn_dims, precision=HIGHEST,
                             preferred_element_type=F32)
    tok = lax.broadcasted_iota(jnp.int32, gate_t.shape, 1)
    acc = jnp.zeros(o_ref.shape, F32)

    def half_rows(t, first):
        return jnp.concatenate([bufs[t][r].reshape(PEER_PAIRS, w).astype(BF16)
                                for r in range(first, first + half)], axis=1)

    def second_stage(t, s, acc):
        a = jnp.where(tok == t, jax.nn.gelu(s) * gate_t, 0.0)
        return acc + lax.dot_general(a.astype(BF16), half_rows(t, half), tn_dims, preferred_element_type=F32)

    s_prev = None
    for t in range(PEER_TOKENS):
        wait(t)
        ahead = t + PEER_LOOKAHEAD
        for k in range(PEER_PAIRS):
            issue(idx_cur if ahead < PEER_TOKENS else idx_nxt, ahead % PEER_TOKENS, k // SUBLANES, k % SUBLANES)
        s = jnp.dot(half_rows(t, 0), hb_t, preferred_element_type=F32)
        if t:
            acc = second_stage(t - 1, s_prev, acc)
        s_prev = s
    o_ref[...] = second_stage(PEER_TOKENS - 1, s_prev, acc)

    @pl.when(i == nsteps - 1)
    def _():
        for t in range(PEER_LOOKAHEAD):
            wait(t)


def peer_experts(h, idx, gate, u, v):
    ntok, dm = h.shape
    assert ntok % PEER_TOKENS == 0 and idx.shape == (ntok, PEER_PAIRS) and dm % (SUBLANES * LANES) == 0
    nsteps = ntok // PEER_TOKENS
    slab, w = 2 * dm // LANES, LANES
    uv = jnp.concatenate([u, v], axis=1).reshape(u.shape[0], slab, w)
    step_idx = PEER_TOKENS * PEER_PAIRS
    tok_block = lambda i: (i, 0)
    return pl.pallas_call(
        functools.partial(_peer_expert_kernel, nsteps=nsteps),
        grid=(nsteps,),
        in_specs=[pl.BlockSpec((step_idx,), lambda i: (i,), memory_space=pltpu.SMEM),
                  pl.BlockSpec((step_idx,), lambda i: (jnp.minimum(i + 1, nsteps - 1),), memory_space=pltpu.SMEM),
                  pl.BlockSpec((PEER_TOKENS, dm), tok_block),
                  pl.BlockSpec((PEER_TOKENS, PEER_PAIRS), tok_block),
                  pl.BlockSpec(memory_space=pl.ANY)],
        out_specs=pl.BlockSpec((PEER_TOKENS, dm), tok_block),
        out_shape=jax.ShapeDtypeStruct((ntok, dm), F32),
        scratch_shapes=[pltpu.VMEM((slab, PEER_PAIRS // SUBLANES, SUBLANES, w), F32)] * PEER_TOKENS
                       + [pltpu.SemaphoreType.DMA((PEER_TOKENS,))],
        compiler_params=pltpu.CompilerParams(dimension_semantics=("arbitrary",)),
        name="peer_experts",
    )(idx.reshape(-1), idx.reshape(-1), h, gate, uv)


def rmsnorm(x, g=None):
    y = x * lax.rsqrt(jnp.mean(x * x, axis=-1, keepdims=True) + EPS)
    return y if g is None else y * g.astype(F32)


def l2norm(x):
    return x * lax.rsqrt(jnp.sum(x * x, axis=-1, keepdims=True) + EPS)


def adaln(cond, w, b):
    n = cond.shape[0]
    cp = jnp.pad(jax.nn.silu(cond), ((0, (-n) % SUBLANES), (0, 0)))
    m = (matmul(cp, w)[:n] + b)[:, None, :]
    return jnp.split(m, N_MOD, axis=-1)


def dwconv(x, w, b=None):
    pad = SHORT_CONV // 2
    y = lax.conv_general_dilated(x, w[:, None, :].astype(x.dtype), window_strides=(1,),
                                 padding=[(pad, pad)], dimension_numbers=('NWC', 'WIO', 'NWC'),
                                 feature_group_count=x.shape[-1])
    return y if b is None else y + b.astype(x.dtype)


def segment_conv(x, n_ctx, w, b=None):
    return jnp.concatenate([dwconv(x[:, :n_ctx], w, b), dwconv(x[:, n_ctx:], w, b)], axis=1)


def axial_rope_tables(rows, dtype):
    row = jnp.repeat(jnp.arange(rows, dtype=F32), GRID_W)
    col = jnp.tile(jnp.arange(GRID_W, dtype=F32), rows)
    nf = A_DIM // 4
    inv = ROPE_THETA ** (-jnp.arange(nf, dtype=F32) / nf)
    ar, ac = row[:, None] * inv, col[:, None] * inv
    return tuple(t.astype(dtype) for t in (jnp.cos(ar), jnp.sin(ar), jnp.cos(ac), jnp.sin(ac)))


def rope_half(x, cos, sin):
    x1, x2 = jnp.split(x, 2, axis=-1)
    cs, sn = cos[:, None, None, :], sin[:, None, None, :]
    return jnp.concatenate([x1 * cs - x2 * sn, x2 * cs + x1 * sn], axis=-1)


def axial_rope(x, tabs):
    cr, sr, cc, sc = tabs
    xr, xc = jnp.split(x, 2, axis=-1)
    return jnp.concatenate([rope_half(xr, cr, sr), rope_half(xc, cc, sc)], axis=-1)


def diff_attention(qkv, n_ctx, rope, layer, need_ctx, qn_g, kn_g, lq1, lk1, lq2, lk2, sub_g):
    lam_init = 0.8 - 0.6 * math.exp(-0.3 * layer)
    f = lambda t: t.astype(F32)
    lam = jnp.exp(jnp.sum(f(lq1) * f(lk1))) - jnp.exp(jnp.sum(f(lq2) * f(lk2))) + lam_init
    q, k, v = jnp.split(qkv, 3, axis=-1)
    bn, tn, _ = q.shape

    def norm_rotate(t, g):
        t = rmsnorm(t.reshape(bn, tn, A_HEADS, 2, A_DIM), g)
        return jnp.concatenate([t[:, :n_ctx], axial_rope(t[:, n_ctx:], rope)], axis=1).reshape(bn, tn, -1)

    q = (norm_rotate(q, qn_g) * (A_DIM ** -0.5)).astype(BF16)
    kt = jnp.swapaxes(norm_rotate(k, kn_g).astype(BF16), 1, 2)
    v = v.astype(BF16)
    scale = 1.0 - lam_init
    yl = diff_attention_core(q[:, n_ctx:], kt, v, lam, sub_g, scale)
    if need_ctx:
        yc = diff_attention_core(q[:, :n_ctx], kt[:, :, :n_ctx], v[:, :n_ctx], lam, sub_g, scale)
    else:
        yc = jnp.zeros((bn, n_ctx, yl.shape[-1]), F32)
    return jnp.concatenate([yc, yl], axis=1)


def gated_deltanet(qkv, z, b_raw, a_raw, n_ctx, conv_w, a_log, dt_bias, norm_g):
    bn, tn, _ = qkv.shape
    qkv = jax.nn.silu(segment_conv(qkv, n_ctx, conv_w))
    q, k, v = jnp.split(qkv, 3, axis=-1)
    heads = lambda t: t.reshape(bn, tn, B_HEADS, B_DIM)
    q = (l2norm(heads(q)) * (B_DIM ** -0.5)).reshape(bn, tn, -1)
    k = l2norm(heads(k)).reshape(bn, tn, -1)
    beta = jax.nn.sigmoid(b_raw)
    g = -jnp.exp(a_log.astype(F32)).reshape(-1) * jax.nn.softplus(a_raw + dt_bias.astype(F32).reshape(-1))
    o = (gdn_scan(q, k, v, g[..., :B_HEADS], beta[..., :B_HEADS], n_ctx, rev=False)
         + gdn_scan(q, k, v, g[..., B_HEADS:], beta[..., B_HEADS:], n_ctx, rev=True))
    y = rmsnorm(heads(o), norm_g) * jax.nn.silu(heads(z))
    return y.reshape(bn, tn, -1)


def token_mixer(h, n_ctx, rope, layer, need_ctx, w_in,
                attn_qn_g, attn_kn_g, lam_q1, lam_k1, lam_q2, lam_k2, attn_sub_g,
                gdn_conv_w, gdn_a_log, gdn_dt_bias, gdn_norm_g,
                lru_conv_w, lru_conv_b, lru_w_r, lru_b_r, lru_w_i, lru_b_i, lru_lambda,
                w_branch, w_out):
    offs = np.cumsum((0,) + IN_SPLITS)
    big = [0, 1, 2, 5, 6, 7]
    w_big = jnp.concatenate([w_in[:, offs[i]:offs[i + 1]] for i in big], axis=1).astype(BF16)
    w_small = jnp.pad(w_in[:, offs[3]:offs[5]], ((0, 0), (0, LANES - 4 * B_HEADS))).astype(BF16)
    big_offs = np.cumsum([IN_SPLITS[i] for i in big])[:-1].tolist()
    p_attn, p_gdn, p_z, p_lru, p_lru_gate, p_merge = jnp.split(matmul3(h, w_big), big_offs, axis=-1)
    p_small = matmul3(h, w_small, tn=LANES)
    b_raw, a_raw = p_small[..., :2 * B_HEADS], p_small[..., 2 * B_HEADS:4 * B_HEADS]

    ya = diff_attention(p_attn, n_ctx, rope, layer, need_ctx, attn_qn_g, attn_kn_g,
                        lam_q1, lam_k1, lam_q2, lam_k2, attn_sub_g)
    yb = gated_deltanet(p_gdn, p_z, b_raw, a_raw, n_ctx, gdn_conv_w, gdn_a_log, gdn_dt_bias, gdn_norm_g)
    yr = lru_scan(segment_conv(p_lru, n_ctx, lru_conv_w, lru_conv_b), p_lru_gate,
                  lru_w_r, lru_b_r, lru_w_i, lru_b_i, lru_lambda, n_ctx)
    wb = w_branch.astype(BF16)
    gates = jnp.split(jax.nn.sigmoid(p_merge), N_BRANCH, axis=-1)
    y = gates[0] * matmul3(ya, wb[0])
    for j, yj in ((1, yb), (2, yr)):
        y = y + gates[j] * matmul3(yj, wb[j])
    return matmul3(y, w_out.astype(BF16))


def peer_ffn(h, wq, subkeys, u, v):
    idx, gate = peer_route(matmul(h, wq.astype(BF16)), subkeys)
    return peer_experts(h, idx, gate, u, v)


def kernel(x, c, ctx, c_ctx, w_ada, b_ada, norm1_g, norm2_g, w_in, attn_qn_g, attn_kn_g, lam_q1, lam_k1, lam_q2, lam_k2, attn_sub_g, gdn_conv_w, gdn_a_log, gdn_dt_bias, gdn_norm_g, lru_conv_w, lru_conv_b, lru_w_r, lru_b_r, lru_w_i, lru_b_i, lru_lambda, w_branch, w_out, peer_wq, peer_subkeys, peer_u, peer_v):
    bn, n_lat, dm = x.shape
    n_ctx = ctx.shape[1]
    rope = axial_rope_tables(n_lat // GRID_W, x.dtype)
    is_ctx = (jnp.arange(n_ctx + n_lat) < n_ctx)[None, :, None]
    xa = jnp.concatenate([ctx, x], axis=1)
    for l in range(DEPTH):
        need_ctx = l < DEPTH - 1
        w_ada_l = w_ada[l].astype(BF16)
        ml = adaln(c, w_ada_l, b_ada[l])
        mc = adaln(c_ctx[None, :], w_ada_l, b_ada[l])
        mod = [jnp.where(is_ctx, mc[j], ml[j]) for j in range(N_MOD)]
        h = rmsnorm(xa, norm1_g[l]) * (1.0 + mod[1]) + mod[0]
        y = token_mixer(h, n_ctx, rope, l, need_ctx, w_in[l],
                        attn_qn_g[l], attn_kn_g[l], lam_q1[l], lam_k1[l], lam_q2[l], lam_k2[l], attn_sub_g[l],
                        gdn_conv_w[l], gdn_a_log[l], gdn_dt_bias[l], gdn_norm_g[l],
                        lru_conv_w[l], lru_conv_b[l], lru_w_r[l], lru_b_r[l], lru_w_i[l], lru_b_i[l], lru_lambda[l],
                        w_branch[l], w_out[l])
        xa = xa + mod[2] * y
        if not need_ctx:
            xa, mod = xa[:, n_ctx:], [ml[j] for j in range(N_MOD)]
        h = rmsnorm(xa, norm2_g[l]) * (1.0 + mod[4]) + mod[3]
        f = peer_ffn(h.reshape(-1, dm), peer_wq[l], peer_subkeys[l], peer_u[l], peer_v[l])
        xa = xa + mod[5] * f.reshape(xa.shape)
    return xa
```

```python
import functools
import math

import jax
import jax.numpy as jnp
import numpy as np
from jax import lax
from jax.experimental import pallas as pl
from jax.experimental.pallas import tpu as pltpu

D_MODEL = 1024
DEPTH = 4
GRID_W = 64
EPS = 1e-6
N_MOD = 6
A_HEADS = 8
A_DIM = 64
ROPE_THETA = 10000.0
B_HEADS = 8
B_DIM = 128
B_CHUNK = 64
SHORT_CONV = 5
C_WIDTH = 1024
C_BLOCKS = 16
LRU_C = 8.0
N_BRANCH = 3
P_HEADS = 8
P_KEYS = 128
P_QDIM = 256
P_TOPK = 16

A_QKV = 3 * A_HEADS * 2 * A_DIM
B_QKV = 3 * B_HEADS * B_DIM
IN_SPLITS = (A_QKV, B_QKV, B_HEADS * B_DIM, 2 * B_HEADS, 2 * B_HEADS, C_WIDTH, C_WIDTH, N_BRANCH * D_MODEL)

F32 = jnp.float32
BF16 = jnp.bfloat16
HIGHEST = lax.Precision.HIGHEST
NT_DIMS = (((1,), (1,)), ((), ()))
SUBLANES = 8
LANES = 128


def _matmul_kernel(x_ref, w_ref, o_ref):
    o_ref[...] = jnp.dot(x_ref[...].astype(BF16), w_ref[...].astype(BF16),
                         preferred_element_type=F32).astype(o_ref.dtype)


def matmul(x, w, *, tm=1024, tn=1024, out_dtype=F32):
    m, k = x.shape
    n = w.shape[1]
    tm, tn = min(tm, m), min(tn, n)
    assert m % tm == 0 and n % tn == 0, (m, n, tm, tn)
    return pl.pallas_call(
        _matmul_kernel,
        grid=(m // tm, n // tn),
        in_specs=[pl.BlockSpec((tm, k), lambda i, j: (i, 0)),
                  pl.BlockSpec((k, tn), lambda i, j: (0, j))],
        out_specs=pl.BlockSpec((tm, tn), lambda i, j: (i, j)),
        out_shape=jax.ShapeDtypeStruct((m, n), out_dtype),
        compiler_params=pltpu.CompilerParams(dimension_semantics=("parallel", "parallel")),
        name="matmul",
    )(x, w)


def matmul3(x, w, **kw):
    b, t, k = x.shape
    return matmul(x.reshape(b * t, k), w, **kw).reshape(b, t, -1)


def _diff_attn_kernel(lam_ref, q_ref, kt_ref, v_ref, g_ref, o_ref, *, out_scale):
    q = q_ref[0]
    kt = kt_ref[0]
    v = v_ref[0]
    lane = lax.broadcasted_iota(jnp.int32, q.shape, 1)
    zero = jnp.zeros_like(q)

    def softmax_v(qc):
        s = jnp.dot(qc, kt, preferred_element_type=F32)
        p = jnp.exp(s - jnp.max(s, axis=-1, keepdims=True))
        l = jnp.sum(p, axis=-1, keepdims=True)
        return jnp.dot(p.astype(BF16), v, preferred_element_type=F32) / l

    o = softmax_v(jnp.where(lane < A_DIM, q, zero)) - lam_ref[0] * softmax_v(jnp.where(lane >= A_DIM, q, zero))
    y = o * lax.rsqrt(jnp.mean(o * o, axis=-1, keepdims=True) + EPS) * g_ref[...]
    o_ref[0] = y * out_scale


def diff_attention_core(q, kt, v, lam, sub_g, out_scale, *, tq=256):
    bn, tq_all, _ = q.shape
    tk = v.shape[1]
    tq = min(tq, tq_all)
    hd = 2 * A_DIM
    return pl.pallas_call(
        functools.partial(_diff_attn_kernel, out_scale=out_scale),
        grid=(bn, A_HEADS, tq_all // tq),
        in_specs=[pl.BlockSpec(memory_space=pltpu.SMEM),
                  pl.BlockSpec((1, tq, hd), lambda b, h, i: (b, i, h)),
                  pl.BlockSpec((1, hd, tk), lambda b, h, i: (b, h, 0)),
                  pl.BlockSpec((1, tk, hd), lambda b, h, i: (b, 0, h)),
                  pl.BlockSpec((1, hd), lambda b, h, i: (0, 0))],
        out_specs=pl.BlockSpec((1, tq, hd), lambda b, h, i: (b, i, h)),
        out_shape=jax.ShapeDtypeStruct((bn, tq_all, A_HEADS * hd), F32),
        compiler_params=pltpu.CompilerParams(dimension_semantics=("parallel", "parallel", "parallel")),
        name="diff_attn",
    )(lam.reshape(1).astype(F32), q, kt, v, sub_g.reshape(1, hd).astype(F32))


BMM = (((2,), (1,)), ((0,), (0,)))
BMM_NT = (((2,), (2,)), ((0,), (0,)))
BMM_TN = (((1,), (1,)), ((0,), (0,)))


def _bdot(a, b, dims=BMM):
    return lax.dot_general(a, b, dims, preferred_element_type=F32)


def _bdot3(a, b):
    a_hi, b_hi = a.astype(BF16), b.astype(BF16)
    a_lo = (a - a_hi.astype(F32)).astype(BF16)
    b_lo = (b - b_hi.astype(F32)).astype(BF16)
    return _bdot(a_hi, b_hi) + (_bdot(a_hi, b_lo) + _bdot(a_lo, b_hi))


def _gdn_kernel(q_ref, k_ref, v_ref, gcol_ref, bcol_ref, grow_ref, o_ref, s_ref, *, rev):
    @pl.when(pl.program_id(1) == 0)
    def _():
        s_ref[...] = jnp.zeros_like(s_ref)

    c = B_CHUNK
    ii = lax.broadcasted_iota(jnp.int32, (c, c), 0)
    jj = lax.broadcasted_iota(jnp.int32, (c, c), 1)
    incl = (ii <= jj) if rev else (ii >= jj)
    strict = (ii < jj) if rev else (ii > jj)
    eye = (ii == jj).astype(F32)
    m_incl = incl.astype(F32)
    last = 0 if rev else c - 1
    hs = range(B_HEADS)

    gc_cols = jnp.dot(m_incl, gcol_ref[0], precision=HIGHEST, preferred_element_type=F32)
    gc_rows = lax.dot_general(grow_ref[0, 0], m_incl, NT_DIMS, precision=HIGHEST, preferred_element_type=F32)
    beta_cols = bcol_ref[0]
    gi = jnp.stack([gc_cols[:, h:h + 1] for h in hs])
    gj = jnp.stack([gc_rows[h:h + 1, :] for h in hs])
    beta = jnp.stack([beta_cols[:, h:h + 1] for h in hs])
    heads = lambda ref: jnp.stack([ref[0, :, h * B_DIM:(h + 1) * B_DIM] for h in hs])
    q, k, v = heads(q_ref), heads(k_ref), heads(v_ref)

    dec = jnp.where(incl, jnp.exp(jnp.where(incl, gi - gj, 0.0)), 0.0)
    kb = k * beta
    kk = _bdot(kb, k, BMM_NT)
    qk = _bdot(q, k, BMM_NT) * dec
    lmat = jnp.where(strict, kk * dec, 0.0)
    same = lambda b: (ii // b) == (jj // b)
    base = 4
    nl = -jnp.where(same(base), lmat, 0.0)
    inv = eye + nl
    inv = inv + _bdot3(inv, _bdot3(nl, nl))
    b = base
    while b < c:
        join = jnp.where(same(2 * b) & jnp.logical_not(same(b)), lmat, 0.0)
        inv = inv - _bdot3(_bdot3(inv, join), inv)
        b *= 2
    eg = jnp.exp(gi)
    uw = _bdot3(inv, jnp.concatenate([v * beta, kb * eg], axis=2))
    u, w = uw[:, :, :B_DIM], uw[:, :, B_DIM:]
    s = s_ref[...]
    v_new = u - _bdot(w, s)
    o = _bdot(q * eg, s) + _bdot(qk, v_new)
    for h in hs:
        o_ref[0, :, h * B_DIM:(h + 1) * B_DIM] = o[h]
    g_last = gi[:, last:last + 1, :]
    kd = k * jnp.exp(g_last - gi)
    s_ref[...] = s * jnp.exp(g_last) + _bdot(kd, v_new, BMM_TN)


def gdn_scan(q, k, v, g, beta, n_ctx, *, rev):
    bn, tn, _ = q.shape
    nc = tn // B_CHUNK
    ncc = n_ctx // B_CHUNK
    g_rows = jnp.swapaxes(g.reshape(bn, nc, B_CHUNK, B_HEADS), 2, 3)

    def chunk(n):
        return jnp.where(n < ncc, ncc - 1 - n, nc - 1 + ncc - n) if rev else n

    tok = lambda b, n: (b, chunk(n), 0)
    wide = pl.BlockSpec((1, B_CHUNK, B_HEADS * B_DIM), tok)
    narrow = pl.BlockSpec((1, B_CHUNK, B_HEADS), tok)
    return pl.pallas_call(
        functools.partial(_gdn_kernel, rev=rev),
        grid=(bn, nc),
        in_specs=[wide, wide, wide, narrow, narrow,
                  pl.BlockSpec((1, 1, B_HEADS, B_CHUNK), lambda b, n: (b, chunk(n), 0, 0))],
        out_specs=wide,
        out_shape=jax.ShapeDtypeStruct((bn, tn, B_HEADS * B_DIM), F32),
        scratch_shapes=[pltpu.VMEM((B_HEADS, B_DIM, B_DIM), F32)],
        compiler_params=pltpu.CompilerParams(dimension_semantics=("parallel", "arbitrary")),
        name="gdn_scan",
    )(q, k, v, g, beta, g_rows)


LRU_CG = 256
LRU_ROWS = 256


def _lru_tile_scan(a, b, h, row, rev):
    for s in (1, 2, 4):
        shift = SUBLANES - s if rev else s
        valid = (row < SUBLANES - s) if rev else (row >= s)
        a_s = pltpu.roll(a, shift, 0)
        b_s = pltpu.roll(b, shift, 0)
        b = jnp.where(valid, a * b_s + b, b)
        a = jnp.where(valid, a * a_s, a)
    ht = a * h + b
    return ht, (ht[0:1] if rev else ht[SUBLANES - 1:SUBLANES])


def _lru_kernel(x_ref, gate_ref, wr_ref, wi_ref, br_ref, bi_ref, c_ref, o_ref, *, n_ctx):
    tn = x_ref.shape[1]
    row = lax.broadcasted_iota(jnp.int32, (SUBLANES, LRU_CG), 0)
    ntile = LRU_ROWS // SUBLANES

    def chunk(r0, h, d, rev):
        xs = x_ref[0, pl.ds(r0, LRU_ROWS), :]
        xb = xs.astype(BF16)
        r = jax.nn.sigmoid(jnp.dot(xb, wr_ref[d, 0], preferred_element_type=F32) + br_ref[d])
        i = jax.nn.sigmoid(jnp.dot(xb, wi_ref[d, 0], preferred_element_type=F32) + bi_ref[d])
        log_a = c_ref[d] * r
        a = jnp.exp(log_a)
        b = jnp.sqrt(jnp.maximum(1.0 - a * a, 0.0)) * (i * xs)
        tiles = [None] * ntile
        for ti in (reversed(range(ntile)) if rev else range(ntile)):
            sl = slice(ti * SUBLANES, (ti + 1) * SUBLANES)
            tiles[ti], h = _lru_tile_scan(a[sl], b[sl], h, row, rev)
        hs = jnp.concatenate(tiles, axis=0)
        if d == 0:
            o_ref[0, pl.ds(r0, LRU_ROWS), :] = hs
        else:
            gate = gate_ref[0, pl.ds(r0, LRU_ROWS), :]
            o_ref[0, pl.ds(r0, LRU_ROWS), :] = (o_ref[0, pl.ds(r0, LRU_ROWS), :] + hs) * jax.nn.gelu(gate)
        return h

    for d, rev in ((0, False), (1, True)):
        h = jnp.zeros((1, LRU_CG), F32)
        for s0, s1 in ((0, n_ctx), (n_ctx, tn)):
            nch = (s1 - s0) // LRU_ROWS

            def body(ci, h, s0=s0, nch=nch, d=d, rev=rev):
                cidx = nch - 1 - ci if rev else ci
                return chunk(pl.multiple_of(s0 + cidx * LRU_ROWS, LRU_ROWS), h, d, rev)

            h = lax.fori_loop(0, nch, body, h)


def lru_scan(x, gate, w_r, b_r, w_i, b_i, lam, n_ctx):
    bn, tn, cw = x.shape
    assert n_ctx % LRU_ROWS == 0 and (tn - n_ctx) % LRU_ROWS == 0 and cw % LRU_CG == 0
    ncg = cw // LRU_CG
    per = LRU_CG // (C_WIDTH // C_BLOCKS)

    def blockdiag_tiles(w):
        blk = w.shape[-1]
        wt = w.astype(F32).reshape(2, ncg, per, blk, blk)
        eye = jnp.eye(per, dtype=F32)
        return jnp.einsum('dgpab,pq->dgpaqb', wt, eye).reshape(2, ncg, LRU_CG, LRU_CG).astype(BF16)

    cvec = (-LRU_C * jax.nn.softplus(-lam.astype(F32))).reshape(2, 1, cw)
    tok = pl.BlockSpec((1, tn, LRU_CG), lambda b, g: (b, 0, g))
    wspec = pl.BlockSpec((2, 1, LRU_CG, LRU_CG), lambda b, g: (0, g, 0, 0))
    vspec = pl.BlockSpec((2, 1, LRU_CG), lambda b, g: (0, 0, g))
    return pl.pallas_call(
        functools.partial(_lru_kernel, n_ctx=n_ctx),
        grid=(bn, ncg),
        in_specs=[tok, tok, wspec, wspec, vspec, vspec, vspec],
        out_specs=tok,
        out_shape=jax.ShapeDtypeStruct((bn, tn, cw), F32),
        compiler_params=pltpu.CompilerParams(dimension_semantics=("parallel", "parallel")),
        name="lru_scan",
    )(x, gate, blockdiag_tiles(w_r), blockdiag_tiles(w_i),
      b_r.astype(F32).reshape(2, 1, cw), b_i.astype(F32).reshape(2, 1, cw), cvec)


ROUTE_TOKENS = 128
PEER_PAIRS = P_HEADS * P_TOPK
N_CAND = P_TOPK * P_TOPK


def _lane_topk(s, k, payload=None):
    g, tb, width = s.shape
    lane = lax.broadcasted_iota(jnp.int32, s.shape, 2).astype(F32)
    out_lane = lax.broadcasted_iota(jnp.int32, (g, tb, LANES), 2)
    vals = jnp.zeros((g, tb, LANES), F32)
    picked = jnp.zeros((g, tb, LANES), F32)
    for r in range(k):
        m = jnp.max(s, axis=-1, keepdims=True)
        p = jnp.min(jnp.where(s == m, lane, float(width)), axis=-1, keepdims=True)
        sel = lane == p
        if payload is not None:
            p = jnp.sum(jnp.where(sel, payload, 0.0), axis=-1, keepdims=True)
        vals = jnp.where(out_lane == r, m, vals)
        picked = jnp.where(out_lane == r, p, picked)
        s = jnp.where(sel, -jnp.inf, s)
    return vals, picked


def _peer_route_kernel(q_ref, sk_ref, idx_ref, gate_ref):
    tb = q_ref.shape[0]
    half = P_QDIM // 2
    er = lax.broadcasted_iota(jnp.int32, (LANES, N_CAND), 0)
    ec = lax.broadcasted_iota(jnp.int32, (LANES, N_CAND), 1)
    e_first = (ec // P_TOPK == er).astype(F32)
    e_second = ((ec % P_TOPK == er) & (er < P_TOPK)).astype(F32)

    def spread(t, e):
        y = jnp.dot(t.reshape(P_HEADS * tb, LANES), e, precision=HIGHEST, preferred_element_type=F32)
        return y.reshape(P_HEADS, tb, N_CAND)

    def scores(c):
        out = []
        for h in range(P_HEADS):
            x = q_ref[:, (2 * h + c) * half:(2 * h + c + 1) * half]
            xn = x * lax.rsqrt(jnp.mean(x * x, axis=-1, keepdims=True) + EPS)
            out.append(lax.dot_general(xn, sk_ref[h, c], NT_DIMS, preferred_element_type=F32))
        return jnp.stack(out)

    v1, i1 = _lane_topk(scores(0), P_TOPK)
    v2, i2 = _lane_topk(scores(1), P_TOPK)
    cand = spread(v1, e_first) + spread(v2, e_second)
    cand_idx = spread(i1, e_first) * float(P_KEYS) + spread(i2, e_second)
    best, eid = _lane_topk(cand, P_TOPK, payload=cand_idx)
    out_lane = lax.broadcasted_iota(jnp.int32, best.shape, 2)
    top = jnp.max(jnp.where(out_lane < P_TOPK, best, -jnp.inf), axis=-1, keepdims=True)
    ex = jnp.where(out_lane < P_TOPK, jnp.exp(best - top), 0.0)
    gate = ex / jnp.sum(ex, axis=-1, keepdims=True)
    place = lambda t, h: pltpu.roll(t[h], h * P_TOPK, 1) if h else t[0]
    idx_ref[...] = sum(place(eid, h) for h in range(P_HEADS)).astype(jnp.int32)
    gate_ref[...] = sum(place(gate, h) for h in range(P_HEADS))


def peer_route(q, subkeys):
    ntok = q.shape[0]
    assert ntok % ROUTE_TOKENS == 0 and PEER_PAIRS == LANES and P_KEYS == LANES
    tok = lambda i: (i, 0)
    return pl.pallas_call(
        _peer_route_kernel,
        grid=(ntok // ROUTE_TOKENS,),
        in_specs=[pl.BlockSpec((ROUTE_TOKENS, q.shape[1]), tok),
                  pl.BlockSpec(subkeys.shape, lambda i: (0, 0, 0, 0))],
        out_specs=[pl.BlockSpec((ROUTE_TOKENS, PEER_PAIRS), tok), pl.BlockSpec((ROUTE_TOKENS, PEER_PAIRS), tok)],
        out_shape=[jax.ShapeDtypeStruct((ntok, PEER_PAIRS), jnp.int32),
                   jax.ShapeDtypeStruct((ntok, PEER_PAIRS), F32)],
        compiler_params=pltpu.CompilerParams(dimension_semantics=("parallel",)),
        name="peer_route",
    )(q, subkeys.astype(F32))


PEER_TOKENS = 8
PEER_LOOKAHEAD = 4


def _peer_expert_kernel(idx_cur, idx_nxt, h_ref, gate_ref, uv_hbm, o_ref, *scratch, nsteps):
    bufs, sem = scratch[:PEER_TOKENS], scratch[PEER_TOKENS]
    i = pl.program_id(0)
    slab, w = uv_hbm.shape[1:]
    half = slab // 2

    def issue(idx_ref, t, g, j):
        pltpu.make_async_copy(uv_hbm.at[idx_ref[t * PEER_PAIRS + g * SUBLANES + j]], bufs[t].at[:, g, j],
                              sem.at[t]).start(priority=j % 2)

    def wait(t):
        pltpu.make_async_copy(bufs[t], bufs[t], sem.at[t]).wait()

    @pl.when(i == 0)
    def _():
        for t in range(PEER_LOOKAHEAD):
            def body(g, carry, t=t):
                for j in range(SUBLANES):
                    issue(idx_cur, t, g, j)
                return carry
            lax.fori_loop(0, PEER_PAIRS // SUBLANES, body, 0)

    eye_t = (lax.broadcasted_iota(jnp.int32, (PEER_TOKENS, PEER_TOKENS), 0)
             == lax.broadcasted_iota(jnp.int32, (PEER_TOKENS, PEER_TOKENS), 1))
    tn_dims = (((0,), (0,)), ((), ()))
    hb_t = lax.dot_general(h_ref[...].astype(BF16), eye_t.astype(BF16), tn_dims,
                           preferred_element_type=F32).astype(BF16)
    gate_t = lax.dot_general(gate_ref[...], eye_t.astype(F32), tn_dims, precision=HIGHEST,
                             preferred_element_type=F32)
    tok = lax.broadcasted_iota(jnp.int32, gate_t.shape, 1)
    acc = jnp.zeros(o_ref.shape, F32)

    def half_rows(t, first):
        return jnp.concatenate([bufs[t][r].reshape(PEER_PAIRS, w).astype(BF16)
                                for r in range(first, first + half)], axis=1)

    def second_stage(t, s, acc):
        a = jnp.where(tok == t, jax.nn.gelu(s) * gate_t, 0.0)
        return acc + lax.dot_general(a.astype(BF16), half_rows(t, half), tn_dims, preferred_element_type=F32)

    s_prev = None
    for t in range(PEER_TOKENS):
        wait(t)
        ahead = t + PEER_LOOKAHEAD
        for k in range(PEER_PAIRS):
            issue(idx_cur if ahead < PEER_TOKENS else idx_nxt, ahead % PEER_TOKENS, k // SUBLANES, k % SUBLANES)
        s = jnp.dot(half_rows(t, 0), hb_t, preferred_element_type=F32)
        if t:
            acc = second_stage(t - 1, s_prev, acc)
        s_prev = s
    o_ref[...] = second_stage(PEER_TOKENS - 1, s_prev, acc)

    @pl.when(i == nsteps - 1)
    def _():
        for t in range(PEER_LOOKAHEAD):
            wait(t)


def peer_experts(h, idx, gate, u, v):
    ntok, dm = h.shape
    assert ntok % PEER_TOKENS == 0 and idx.shape == (ntok, PEER_PAIRS) and dm % (SUBLANES * LANES) == 0
    nsteps = ntok // PEER_TOKENS
    slab, w = 2 * dm // LANES, LANES
    uv = jnp.concatenate([u, v], axis=1).reshape(u.shape[0], slab, w)
    step_idx = PEER_TOKENS * PEER_PAIRS
    tok_block = lambda i: (i, 0)
    return pl.pallas_call(
        functools.partial(_peer_expert_kernel, nsteps=nsteps),
        grid=(nsteps,),
        in_specs=[pl.BlockSpec((step_idx,), lambda i: (i,), memory_space=pltpu.SMEM),
                  pl.BlockSpec((step_idx,), lambda i: (jnp.minimum(i + 1, nsteps - 1),), memory_space=pltpu.SMEM),
                  pl.BlockSpec((PEER_TOKENS, dm), tok_block),
                  pl.BlockSpec((PEER_TOKENS, PEER_PAIRS), tok_block),
                  pl.BlockSpec(memory_space=pl.ANY)],
        out_specs=pl.BlockSpec((PEER_TOKENS, dm), tok_block),
        out_shape=jax.ShapeDtypeStruct((ntok, dm), F32),
        scratch_shapes=[pltpu.VMEM((slab, PEER_PAIRS // SUBLANES, SUBLANES, w), F32)] * PEER_TOKENS
                       + [pltpu.SemaphoreType.DMA((PEER_TOKENS,))],
        compiler_params=pltpu.CompilerParams(dimension_semantics=("arbitrary",)),
        name="peer_experts",
    )(idx.reshape(-1), idx.reshape(-1), h, gate, uv)


def rmsnorm(x, g=None):
    y = x * lax.rsqrt(jnp.mean(x * x, axis=-1, keepdims=True) + EPS)
    return y if g is None else y * g.astype(F32)


def l2norm(x):
    return x * lax.rsqrt(jnp.sum(x * x, axis=-1, keepdims=True) + EPS)


def adaln(cond, w, b):
    n = cond.shape[0]
    cp = jnp.pad(jax.nn.silu(cond), ((0, (-n) % SUBLANES), (0, 0)))
    m = (matmul(cp, w)[:n] + b)[:, None, :]
    return jnp.split(m, N_MOD, axis=-1)


def dwconv(x, w, b=None):
    pad = SHORT_CONV // 2
    y = lax.conv_general_dilated(x, w[:, None, :].astype(x.dtype), window_strides=(1,),
                                 padding=[(pad, pad)], dimension_numbers=('NWC', 'WIO', 'NWC'),
                                 feature_group_count=x.shape[-1])
    return y if b is None else y + b.astype(x.dtype)


def segment_conv(x, n_ctx, w, b=None):
    return jnp.concatenate([dwconv(x[:, :n_ctx], w, b), dwconv(x[:, n_ctx:], w, b)], axis=1)


def axial_rope_tables(rows, dtype):
    row = jnp.repeat(jnp.arange(rows, dtype=F32), GRID_W)
    col = jnp.tile(jnp.arange(GRID_W, dtype=F32), rows)
    nf = A_DIM // 4
    inv = ROPE_THETA ** (-jnp.arange(nf, dtype=F32) / nf)
    ar, ac = row[:, None] * inv, col[:, None] * inv
    return tuple(t.astype(dtype) for t in (jnp.cos(ar), jnp.sin(ar), jnp.cos(ac), jnp.sin(ac)))


def rope_half(x, cos, sin):
    x1, x2 = jnp.split(x, 2, axis=-1)
    cs, sn = cos[:, None, None, :], sin[:, None, None, :]
    return jnp.concatenate([x1 * cs - x2 * sn, x2 * cs + x1 * sn], axis=-1)


def axial_rope(x, tabs):
    cr, sr, cc, sc = tabs
    xr, xc = jnp.split(x, 2, axis=-1)
    return jnp.concatenate([rope_half(xr, cr, sr), rope_half(xc, cc, sc)], axis=-1)


def diff_attention(qkv, n_ctx, rope, layer, need_ctx, qn_g, kn_g, lq1, lk1, lq2, lk2, sub_g):
    lam_init = 0.8 - 0.6 * math.exp(-0.3 * layer)
    f = lambda t: t.astype(F32)
    lam = jnp.exp(jnp.sum(f(lq1) * f(lk1))) - jnp.exp(jnp.sum(f(lq2) * f(lk2))) + lam_init
    q, k, v = jnp.split(qkv, 3, axis=-1)
    bn, tn, _ = q.shape

    def norm_rotate(t, g):
        t = rmsnorm(t.reshape(bn, tn, A_HEADS, 2, A_DIM), g)
        return jnp.concatenate([t[:, :n_ctx], axial_rope(t[:, n_ctx:], rope)], axis=1).reshape(bn, tn, -1)

    q = (norm_rotate(q, qn_g) * (A_DIM ** -0.5)).astype(BF16)
    kt = jnp.swapaxes(norm_rotate(k, kn_g).astype(BF16), 1, 2)
    v = v.astype(BF16)
    scale = 1.0 - lam_init
    yl = diff_attention_core(q[:, n_ctx:], kt, v, lam, sub_g, scale)
    if need_ctx:
        yc = diff_attention_core(q[:, :n_ctx], kt[:, :, :n_ctx], v[:, :n_ctx], lam, sub_g, scale)
    else:
        yc = jnp.zeros((bn, n_ctx, yl.shape[-1]), F32)
    return jnp.concatenate([yc, yl], axis=1)


def gated_deltanet(qkv, z, b_raw, a_raw, n_ctx, conv_w, a_log, dt_bias, norm_g):
    bn, tn, _ = qkv.shape
    qkv = jax.nn.silu(segment_conv(qkv, n_ctx, conv_w))
    q, k, v = jnp.split(qkv, 3, axis=-1)
    heads = lambda t: t.reshape(bn, tn, B_HEADS, B_DIM)
    q = (l2norm(heads(q)) * (B_DIM ** -0.5)).reshape(bn, tn, -1)
    k = l2norm(heads(k)).reshape(bn, tn, -1)
    beta = jax.nn.sigmoid(b_raw)
    g = -jnp.exp(a_log.astype(F32)).reshape(-1) * jax.nn.softplus(a_raw + dt_bias.astype(F32).reshape(-1))
    o = (gdn_scan(q, k, v, g[..., :B_HEADS], beta[..., :B_HEADS], n_ctx, rev=False)
         + gdn_scan(q, k, v, g[..., B_HEADS:], beta[..., B_HEADS:], n_ctx, rev=True))
    y = rmsnorm(heads(o), norm_g) * jax.nn.silu(heads(z))
    return y.reshape(bn, tn, -1)


def token_mixer(h, n_ctx, rope, layer, need_ctx, w_in,
                attn_qn_g, attn_kn_g, lam_q1, lam_k1, lam_q2, lam_k2, attn_sub_g,
                gdn_conv_w, gdn_a_log, gdn_dt_bias, gdn_norm_g,
                lru_conv_w, lru_conv_b, lru_w_r, lru_b_r, lru_w_i, lru_b_i, lru_lambda,
                w_branch, w_out):
    offs = np.cumsum((0,) + IN_SPLITS)
    big = [0, 1, 2, 5, 6, 7]
    w_big = jnp.concatenate([w_in[:, offs[i]:offs[i + 1]] for i in big], axis=1).astype(BF16)
    w_small = jnp.pad(w_in[:, offs[3]:offs[5]], ((0, 0), (0, LANES - 4 * B_HEADS))).astype(BF16)
    big_offs = np.cumsum([IN_SPLITS[i] for i in big])[:-1].tolist()
    p_attn, p_gdn, p_z, p_lru, p_lru_gate, p_merge = jnp.split(matmul3(h, w_big), big_offs, axis=-1)
    p_small = matmul3(h, w_small, tn=LANES)
    b_raw, a_raw = p_small[..., :2 * B_HEADS], p_small[..., 2 * B_HEADS:4 * B_HEADS]

    ya = diff_attention(p_attn, n_ctx, rope, layer, need_ctx, attn_qn_g, attn_kn_g,
                        lam_q1, lam_k1, lam_q2, lam_k2, attn_sub_g)
    yb = gated_deltanet(p_gdn, p_z, b_raw, a_raw, n_ctx, gdn_conv_w, gdn_a_log, gdn_dt_bias, gdn_norm_g)
    yr = lru_scan(segment_conv(p_lru, n_ctx, lru_conv_w, lru_conv_b), p_lru_gate,
                  lru_w_r, lru_b_r, lru_w_i, lru_b_i, lru_lambda, n_ctx)
    wb = w_branch.astype(BF16)
    gates = jnp.split(jax.nn.sigmoid(p_merge), N_BRANCH, axis=-1)
    y = gates[0] * matmul3(ya, wb[0])
    for j, yj in ((1, yb), (2, yr)):
        y = y + gates[j] * matmul3(yj, wb[j])
    return matmul3(y, w_out.astype(BF16))


def peer_ffn(h, wq, subkeys, u, v):
    idx, gate = peer_route(matmul(h, wq.astype(BF16)), subkeys)
    return peer_experts(h, idx, gate, u, v)


def kernel(x, c, ctx, c_ctx, w_ada, b_ada, norm1_g, norm2_g, w_in, attn_qn_g, attn_kn_g, lam_q1, lam_k1, lam_q2, lam_k2, attn_sub_g, gdn_conv_w, gdn_a_log, gdn_dt_bias, gdn_norm_g, lru_conv_w, lru_conv_b, lru_w_r, lru_b_r, lru_w_i, lru_b_i, lru_lambda, w_branch, w_out, peer_wq, peer_subkeys, peer_u, peer_v):
    bn, n_lat, dm = x.shape
    n_ctx = ctx.shape[1]
    rope = axial_rope_tables(n_lat // GRID_W, x.dtype)
    is_ctx = (jnp.arange(n_ctx + n_lat) < n_ctx)[None, :, None]
    xa = jnp.concatenate([ctx, x], axis=1)
    for l in range(DEPTH):
        need_ctx = l < DEPTH - 1
        w_ada_l = w_ada[l].astype(BF16)
        ml = adaln(c, w_ada_l, b_ada[l])
        mc = adaln(c_ctx[None, :], w_ada_l, b_ada[l])
        mod = [jnp.where(is_ctx, mc[j], ml[j]) for j in range(N_MOD)]
        h = rmsnorm(xa, norm1_g[l]) * (1.0 + mod[1]) + mod[0]
        y = token_mixer(h, n_ctx, rope, l, need_ctx, w_in[l],
                        attn_qn_g[l], attn_kn_g[l], lam_q1[l], lam_k1[l], lam_q2[l], lam_k2[l], attn_sub_g[l],
                        gdn_conv_w[l], gdn_a_log[l], gdn_dt_bias[l], gdn_norm_g[l],
                        lru_conv_w[l], lru_conv_b[l], lru_w_r[l], lru_b_r[l], lru_w_i[l], lru_b_i[l], lru_lambda[l],
                        w_branch[l], w_out[l])
        xa = xa + mod[2] * y
        if not need_ctx:
            xa, mod = xa[:, n_ctx:], [ml[j] for j in range(N_MOD)]
        h = rmsnorm(xa, norm2_g[l]) * (1.0 + mod[4]) + mod[3]
        f = peer_ffn(h.reshape(-1, dm), peer_wq[l], peer_subkeys[l], peer_u[l], peer_v[l])
        xa = xa + mod[5] * f.reshape(xa.shape)
    return xa
```

```python
import functools
import math

import jax
import jax.numpy as jnp
import numpy as np
from jax import lax
from jax.experimental import pallas as pl
from jax.experimental.pallas import tpu as pltpu

D_MODEL = 1024
DEPTH = 4
GRID_W = 64
EPS = 1e-6
N_MOD = 6
A_HEADS = 8
A_DIM = 64
ROPE_THETA = 10000.0
B_HEADS = 8
B_DIM = 128
B_CHUNK = 64
SHORT_CONV = 5
C_WIDTH = 1024
C_BLOCKS = 16
LRU_C = 8.0
N_BRANCH = 3
P_HEADS = 8
P_KEYS = 128
P_QDIM = 256
P_TOPK = 16

A_QKV = 3 * A_HEADS * 2 * A_DIM
B_QKV = 3 * B_HEADS * B_DIM
IN_SPLITS = (A_QKV, B_QKV, B_HEADS * B_DIM, 2 * B_HEADS, 2 * B_HEADS, C_WIDTH, C_WIDTH, N_BRANCH * D_MODEL)

F32 = jnp.float32
BF16 = jnp.bfloat16
HIGHEST = lax.Precision.HIGHEST
NT_DIMS = (((1,), (1,)), ((), ()))
SUBLANES = 8
LANES = 128


def _matmul_kernel(x_ref, w_ref, o_ref):
    o_ref[...] = jnp.dot(x_ref[...].astype(BF16), w_ref[...].astype(BF16),
                         preferred_element_type=F32).astype(o_ref.dtype)


def matmul(x, w, *, tm=1024, tn=1024, out_dtype=F32):
    m, k = x.shape
    n = w.shape[1]
    tm, tn = min(tm, m), min(tn, n)
    assert m % tm == 0 and n % tn == 0, (m, n, tm, tn)
    return pl.pallas_call(
        _matmul_kernel,
        grid=(m // tm, n // tn),
        in_specs=[pl.BlockSpec((tm, k), lambda i, j: (i, 0)),
                  pl.BlockSpec((k, tn), lambda i, j: (0, j))],
        out_specs=pl.BlockSpec((tm, tn), lambda i, j: (i, j)),
        out_shape=jax.ShapeDtypeStruct((m, n), out_dtype),
        compiler_params=pltpu.CompilerParams(dimension_semantics=("parallel", "parallel")),
        name="matmul",
    )(x, w)


def matmul3(x, w, **kw):
    b, t, k = x.shape
    return matmul(x.reshape(b * t, k), w, **kw).reshape(b, t, -1)


def _diff_attn_kernel(lam_ref, q_ref, kt_ref, v_ref, g_ref, o_ref, *, out_scale):
    q = q_ref[0]
    kt = kt_ref[0]
    v = v_ref[0]
    lane = lax.broadcasted_iota(jnp.int32, q.shape, 1)
    zero = jnp.zeros_like(q)

    def softmax_v(qc):
        s = jnp.dot(qc, kt, preferred_element_type=F32)
        p = jnp.exp(s - jnp.max(s, axis=-1, keepdims=True))
        l = jnp.sum(p, axis=-1, keepdims=True)
        return jnp.dot(p.astype(BF16), v, preferred_element_type=F32) / l

    o = softmax_v(jnp.where(lane < A_DIM, q, zero)) - lam_ref[0] * softmax_v(jnp.where(lane >= A_DIM, q, zero))
    y = o * lax.rsqrt(jnp.mean(o * o, axis=-1, keepdims=True) + EPS) * g_ref[...]
    o_ref[0] = y * out_scale


def diff_attention_core(q, kt, v, lam, sub_g, out_scale, *, tq=256):
    bn, tq_all, _ = q.shape
    tk = v.shape[1]
    tq = min(tq, tq_all)
    hd = 2 * A_DIM
    return pl.pallas_call(
        functools.partial(_diff_attn_kernel, out_scale=out_scale),
        grid=(bn, A_HEADS, tq_all // tq),
        in_specs=[pl.BlockSpec(memory_space=pltpu.SMEM),
                  pl.BlockSpec((1, tq, hd), lambda b, h, i: (b, i, h)),
                  pl.BlockSpec((1, hd, tk), lambda b, h, i: (b, h, 0)),
                  pl.BlockSpec((1, tk, hd), lambda b, h, i: (b, 0, h)),
                  pl.BlockSpec((1, hd), lambda b, h, i: (0, 0))],
        out_specs=pl.BlockSpec((1, tq, hd), lambda b, h, i: (b, i, h)),
        out_shape=jax.ShapeDtypeStruct((bn, tq_all, A_HEADS * hd), F32),
        compiler_params=pltpu.CompilerParams(dimension_semantics=("parallel", "parallel", "parallel")),
        name="diff_attn",
    )(lam.reshape(1).astype(F32), q, kt, v, sub_g.reshape(1, hd).astype(F32))


BMM = (((2,), (1,)), ((0,), (0,)))
BMM_NT = (((2,), (2,)), ((0,), (0,)))
BMM_TN = (((1,), (1,)), ((0,), (0,)))


def _bdot(a, b, dims=BMM):
    return lax.dot_general(a, b, dims, preferred_element_type=F32)


def _bdot3(a, b):
    a_hi, b_hi = a.astype(BF16), b.astype(BF16)
    a_lo = (a - a_hi.astype(F32)).astype(BF16)
    b_lo = (b - b_hi.astype(F32)).astype(BF16)
    return _bdot(a_hi, b_hi) + (_bdot(a_hi, b_lo) + _bdot(a_lo, b_hi))


def _gdn_kernel(q_ref, k_ref, v_ref, gcol_ref, bcol_ref, grow_ref, o_ref, s_ref, *, rev):
    @pl.when(pl.program_id(1) == 0)
    def _():
        s_ref[...] = jnp.zeros_like(s_ref)

    c = B_CHUNK
    ii = lax.broadcasted_iota(jnp.int32, (c, c), 0)
    jj = lax.broadcasted_iota(jnp.int32, (c, c), 1)
    incl = (ii <= jj) if rev else (ii >= jj)
    strict = (ii < jj) if rev else (ii > jj)
    eye = (ii == jj).astype(F32)
    m_incl = incl.astype(F32)
    last = 0 if rev else c - 1
    hs = range(B_HEADS)

    gc_cols = jnp.dot(m_incl, gcol_ref[0], precision=HIGHEST, preferred_element_type=F32)
    gc_rows = lax.dot_general(grow_ref[0, 0], m_incl, NT_DIMS, precision=HIGHEST, preferred_element_type=F32)
    beta_cols = bcol_ref[0]
    gi = jnp.stack([gc_cols[:, h:h + 1] for h in hs])
    gj = jnp.stack([gc_rows[h:h + 1, :] for h in hs])
    beta = jnp.stack([beta_cols[:, h:h + 1] for h in hs])
    heads = lambda ref: jnp.stack([ref[0, :, h * B_DIM:(h + 1) * B_DIM] for h in hs])
    q, k, v = heads(q_ref), heads(k_ref), heads(v_ref)

    dec = jnp.where(incl, jnp.exp(jnp.where(incl, gi - gj, 0.0)), 0.0)
    kb = k * beta
    kk = _bdot(kb, k, BMM_NT)
    qk = _bdot(q, k, BMM_NT) * dec
    lmat = jnp.where(strict, kk * dec, 0.0)
    same = lambda b: (ii // b) == (jj // b)
    base = 4
    nl = -jnp.where(same(base), lmat, 0.0)
    inv = eye + nl
    inv = inv + _bdot3(inv, _bdot3(nl, nl))
    b = base
    while b < c:
        join = jnp.where(same(2 * b) & jnp.logical_not(same(b)), lmat, 0.0)
        inv = inv - _bdot3(_bdot3(inv, join), inv)
        b *= 2
    eg = jnp.exp(gi)
    uw = _bdot3(inv, jnp.concatenate([v * beta, kb * eg], axis=2))
    u, w = uw[:, :, :B_DIM], uw[:, :, B_DIM:]
    s = s_ref[...]
    v_new = u - _bdot(w, s)
    o = _bdot(q * eg, s) + _bdot(qk, v_new)
    for h in hs:
        o_ref[0, :, h * B_DIM:(h + 1) * B_DIM] = o[h]
    g_last = gi[:, last:last + 1, :]
    kd = k * jnp.exp(g_last - gi)
    s_ref[...] = s * jnp.exp(g_last) + _bdot(kd, v_new, BMM_TN)


def gdn_scan(q, k, v, g, beta, n_ctx, *, rev):
    bn, tn, _ = q.shape
    nc = tn // B_CHUNK
    ncc = n_ctx // B_CHUNK
    g_rows = jnp.swapaxes(g.reshape(bn, nc, B_CHUNK, B_HEADS), 2, 3)

    def chunk(n):
        return jnp.where(n < ncc, ncc - 1 - n, nc - 1 + ncc - n) if rev else n

    tok = lambda b, n: (b, chunk(n), 0)
    wide = pl.BlockSpec((1, B_CHUNK, B_HEADS * B_DIM), tok)
    narrow = pl.BlockSpec((1, B_CHUNK, B_HEADS), tok)
    return pl.pallas_call(
        functools.partial(_gdn_kernel, rev=rev),
        grid=(bn, nc),
        in_specs=[wide, wide, wide, narrow, narrow,
                  pl.BlockSpec((1, 1, B_HEADS, B_CHUNK), lambda b, n: (b, chunk(n), 0, 0))],
        out_specs=wide,
        out_shape=jax.ShapeDtypeStruct((bn, tn, B_HEADS * B_DIM), F32),
        scratch_shapes=[pltpu.VMEM((B_HEADS, B_DIM, B_DIM), F32)],
        compiler_params=pltpu.CompilerParams(dimension_semantics=("parallel", "arbitrary")),
        name="gdn_scan",
    )(q, k, v, g, beta, g_rows)


LRU_CG = 256
LRU_ROWS = 256


def _lru_tile_scan(a, b, h, row, rev):
    for s in (1, 2, 4):
        shift = SUBLANES - s if rev else s
        valid = (row < SUBLANES - s) if rev else (row >= s)
        a_s = pltpu.roll(a, shift, 0)
        b_s = pltpu.roll(b, shift, 0)
        b = jnp.where(valid, a * b_s + b, b)
        a = jnp.where(valid, a * a_s, a)
    ht = a * h + b
    return ht, (ht[0:1] if rev else ht[SUBLANES - 1:SUBLANES])


def _lru_kernel(x_ref, gate_ref, wr_ref, wi_ref, br_ref, bi_ref, c_ref, o_ref, *, n_ctx):
    tn = x_ref.shape[1]
    row = lax.broadcasted_iota(jnp.int32, (SUBLANES, LRU_CG), 0)
    ntile = LRU_ROWS // SUBLANES

    def chunk(r0, h, d, rev):
        xs = x_ref[0, pl.ds(r0, LRU_ROWS), :]
        xb = xs.astype(BF16)
        r = jax.nn.sigmoid(jnp.dot(xb, wr_ref[d, 0], preferred_element_type=F32) + br_ref[d])
        i = jax.nn.sigmoid(jnp.dot(xb, wi_ref[d, 0], preferred_element_type=F32) + bi_ref[d])
        log_a = c_ref[d] * r
        a = jnp.exp(log_a)
        b = jnp.sqrt(jnp.maximum(1.0 - a * a, 0.0)) * (i * xs)
        tiles = [None] * ntile
        for ti in (reversed(range(ntile)) if rev else range(ntile)):
            sl = slice(ti * SUBLANES, (ti + 1) * SUBLANES)
            tiles[ti], h = _lru_tile_scan(a[sl], b[sl], h, row, rev)
        hs = jnp.concatenate(tiles, axis=0)
        if d == 0:
            o_ref[0, pl.ds(r0, LRU_ROWS), :] = hs
        else:
            gate = gate_ref[0, pl.ds(r0, LRU_ROWS), :]
            o_ref[0, pl.ds(r0, LRU_ROWS), :] = (o_ref[0, pl.ds(r0, LRU_ROWS), :] + hs) * jax.nn.gelu(gate)
        return h

    for d, rev in ((0, False), (1, True)):
        h = jnp.zeros((1, LRU_CG), F32)
        for s0, s1 in ((0, n_ctx), (n_ctx, tn)):
            nch = (s1 - s0) // LRU_ROWS

            def body(ci, h, s0=s0, nch=nch, d=d, rev=rev):
                cidx = nch - 1 - ci if rev else ci
                return chunk(pl.multiple_of(s0 + cidx * LRU_ROWS, LRU_ROWS), h, d, rev)

            h = lax.fori_loop(0, nch, body, h)


def lru_scan(x, gate, w_r, b_r, w_i, b_i, lam, n_ctx):
    bn, tn, cw = x.shape
    assert n_ctx % LRU_ROWS == 0 and (tn - n_ctx) % LRU_ROWS == 0 and cw % LRU_CG == 0
    ncg = cw // LRU_CG
    per = LRU_CG // (C_WIDTH // C_BLOCKS)

    def blockdiag_tiles(w):
        blk = w.shape[-1]
        wt = w.astype(F32).reshape(2, ncg, per, blk, blk)
        eye = jnp.eye(per, dtype=F32)
        return jnp.einsum('dgpab,pq->dgpaqb', wt, eye).reshape(2, ncg, LRU_CG, LRU_CG).astype(BF16)

    cvec = (-LRU_C * jax.nn.softplus(-lam.astype(F32))).reshape(2, 1, cw)
    tok = pl.BlockSpec((1, tn, LRU_CG), lambda b, g: (b, 0, g))
    wspec = pl.BlockSpec((2, 1, LRU_CG, LRU_CG), lambda b, g: (0, g, 0, 0))
    vspec = pl.BlockSpec((2, 1, LRU_CG), lambda b, g: (0, 0, g))
    return pl.pallas_call(
        functools.partial(_lru_kernel, n_ctx=n_ctx),
        grid=(bn, ncg),
        in_specs=[tok, tok, wspec, wspec, vspec, vspec, vspec],
        out_specs=tok,
        out_shape=jax.ShapeDtypeStruct((bn, tn, cw), F32),
        compiler_params=pltpu.CompilerParams(dimension_semantics=("parallel", "parallel")),
        name="lru_scan",
    )(x, gate, blockdiag_tiles(w_r), blockdiag_tiles(w_i),
      b_r.astype(F32).reshape(2, 1, cw), b_i.astype(F32).reshape(2, 1, cw), cvec)


ROUTE_TOKENS = 128
PEER_PAIRS = P_HEADS * P_TOPK
N_CAND = P_TOPK * P_TOPK


def _lane_topk(s, k, payload=None):
    g, tb, width = s.shape
    lane = lax.broadcasted_iota(jnp.int32, s.shape, 2).astype(F32)
    out_lane = lax.broadcasted_iota(jnp.int32, (g, tb, LANES), 2)
    vals = jnp.zeros((g, tb, LANES), F32)
    picked = jnp.zeros((g, tb, LANES), F32)
    for r in range(k):
        m = jnp.max(s, axis=-1, keepdims=True)
        p = jnp.min(jnp.where(s == m, lane, float(width)), axis=-1, keepdims=True)
        sel = lane == p
        if payload is not None:
            p = jnp.sum(jnp.where(sel, payload, 0.0), axis=-1, keepdims=True)
        vals = jnp.where(out_lane == r, m, vals)
        picked = jnp.where(out_lane == r, p, picked)
        s = jnp.where(sel, -jnp.inf, s)
    return vals, picked


def _peer_route_kernel(q_ref, sk_ref, idx_ref, gate_ref):
    tb = q_ref.shape[0]
    half = P_QDIM // 2
    er = lax.broadcasted_iota(jnp.int32, (LANES, N_CAND), 0)
    ec = lax.broadcasted_iota(jnp.int32, (LANES, N_CAND), 1)
    e_first = (ec // P_TOPK == er).astype(F32)
    e_second = ((ec % P_TOPK == er) & (er < P_TOPK)).astype(F32)

    def spread(t, e):
        y = jnp.dot(t.reshape(P_HEADS * tb, LANES), e, precision=HIGHEST, preferred_element_type=F32)
        return y.reshape(P_HEADS, tb, N_CAND)

    def scores(c):
        out = []
        for h in range(P_HEADS):
            x = q_ref[:, (2 * h + c) * half:(2 * h + c + 1) * half]
            xn = x * lax.rsqrt(jnp.mean(x * x, axis=-1, keepdims=True) + EPS)
            out.append(lax.dot_general(xn, sk_ref[h, c], NT_DIMS, preferred_element_type=F32))
        return jnp.stack(out)

    v1, i1 = _lane_topk(scores(0), P_TOPK)
    v2, i2 = _lane_topk(scores(1), P_TOPK)
    cand = spread(v1, e_first) + spread(v2, e_second)
    cand_idx = spread(i1, e_first) * float(P_KEYS) + spread(i2, e_second)
    best, eid = _lane_topk(cand, P_TOPK, payload=cand_idx)
    out_lane = lax.broadcasted_iota(jnp.int32, best.shape, 2)
    top = jnp.max(jnp.where(out_lane < P_TOPK, best, -jnp.inf), axis=-1, keepdims=True)
    ex = jnp.where(out_lane < P_TOPK, jnp.exp(best - top), 0.0)
    gate = ex / jnp.sum(ex, axis=-1, keepdims=True)
    place = lambda t, h: pltpu.roll(t[h], h * P_TOPK, 1) if h else t[0]
    idx_ref[...] = sum(place(eid, h) for h in range(P_HEADS)).astype(jnp.int32)
    gate_ref[...] = sum(place(gate, h) for h in range(P_HEADS))


def peer_route(q, subkeys):
    ntok = q.shape[0]
    assert ntok % ROUTE_TOKENS == 0 and PEER_PAIRS == LANES and P_KEYS == LANES
    tok = lambda i: (i, 0)
    return pl.pallas_call(
        _peer_route_kernel,
        grid=(ntok // ROUTE_TOKENS,),
        in_specs=[pl.BlockSpec((ROUTE_TOKENS, q.shape[1]), tok),
                  pl.BlockSpec(subkeys.shape, lambda i: (0, 0, 0, 0))],
        out_specs=[pl.BlockSpec((ROUTE_TOKENS, PEER_PAIRS), tok), pl.BlockSpec((ROUTE_TOKENS, PEER_PAIRS), tok)],
        out_shape=[jax.ShapeDtypeStruct((ntok, PEER_PAIRS), jnp.int32),
                   jax.ShapeDtypeStruct((ntok, PEER_PAIRS), F32)],
        compiler_params=pltpu.CompilerParams(dimension_semantics=("parallel",)),
        name="peer_route",
    )(q, subkeys.astype(F32))


PEER_TOKENS = 8
PEER_LOOKAHEAD = 4


def _peer_expert_kernel(idx_cur, idx_nxt, h_ref, gate_ref, uv_hbm, o_ref, *scratch, nsteps):
    bufs, sem = scratch[:PEER_TOKENS], scratch[PEER_TOKENS]
    i = pl.program_id(0)
    slab, w = uv_hbm.shape[1:]
    half = slab // 2

    def issue(idx_ref, t, g, j):
        pltpu.make_async_copy(uv_hbm.at[idx_ref[t * PEER_PAIRS + g * SUBLANES + j]], bufs[t].at[:, g, j],
                              sem.at[t]).start(priority=j % 2)

    def wait(t):
        pltpu.make_async_copy(bufs[t], bufs[t], sem.at[t]).wait()

    @pl.when(i == 0)
    def _():
        for t in range(PEER_LOOKAHEAD):
            def body(g, carry, t=t):
                for j in range(SUBLANES):
                    issue(idx_cur, t, g, j)
                return carry
            lax.fori_loop(0, PEER_PAIRS // SUBLANES, body, 0)

    eye_t = (lax.broadcasted_iota(jnp.int32, (PEER_TOKENS, PEER_TOKENS), 0)
             == lax.broadcasted_iota(jnp.int32, (PEER_TOKENS, PEER_TOKENS), 1))
    tn_dims = (((0,), (0,)), ((), ()))
    hb_t = lax.dot_general(h_ref[...].astype(BF16), eye_t.astype(BF16), tn_dims,
                           preferred_element_type=F32).astype(BF16)
    gate_t = lax.dot_general(gate_ref[...], eye_t.astype(F32), tn_dims, precision=HIGHEST,
                             preferred_element_type=F32)
    tok = lax.broadcasted_iota(jnp.int32, gate_t.shape, 1)
    acc = jnp.zeros(o_ref.shape, F32)

    def half_rows(t, first):
        parts = []
        for r in range(first, first + half):
            words = bufs[t][r].reshape(PEER_PAIRS, w)
            parts.append(lax.bitcast_convert_type(words << 16, F32).astype(BF16))
            parts.append(lax.bitcast_convert_type(words & jnp.uint32(0xFFFF0000), F32).astype(BF16))
        return jnp.concatenate(parts, axis=1)

    def second_stage(t, s, acc):
        a = jnp.where(tok == t, jax.nn.gelu(s) * gate_t, 0.0)
        return acc + lax.dot_general(a.astype(BF16), half_rows(t, half), tn_dims, preferred_element_type=F32)

    s_prev = None
    for t in range(PEER_TOKENS):
        wait(t)
        ahead = t + PEER_LOOKAHEAD
        for k in range(PEER_PAIRS):
            issue(idx_cur if ahead < PEER_TOKENS else idx_nxt, ahead % PEER_TOKENS, k // SUBLANES, k % SUBLANES)
        s = jnp.dot(half_rows(t, 0), hb_t, preferred_element_type=F32)
        if t:
            acc = second_stage(t - 1, s_prev, acc)
        s_prev = s
    o_ref[...] = second_stage(PEER_TOKENS - 1, s_prev, acc)

    @pl.when(i == nsteps - 1)
    def _():
        for t in range(PEER_LOOKAHEAD):
            wait(t)


def peer_experts(h, idx, gate, u, v):
    ntok, dm = h.shape
    assert ntok % PEER_TOKENS == 0 and idx.shape == (ntok, PEER_PAIRS) and dm % (SUBLANES * LANES) == 0
    nsteps = ntok // PEER_TOKENS
    slab, w = dm // LANES, LANES
    pairs = jnp.concatenate([u, v], axis=1).astype(BF16).reshape(u.shape[0], slab, 2, w)
    uv = lax.bitcast_convert_type(jnp.swapaxes(pairs, 2, 3), jnp.uint32)
    step_idx = PEER_TOKENS * PEER_PAIRS
    tok_block = lambda i: (i, 0)
    return pl.pallas_call(
        functools.partial(_peer_expert_kernel, nsteps=nsteps),
        grid=(nsteps,),
        in_specs=[pl.BlockSpec((step_idx,), lambda i: (i,), memory_space=pltpu.SMEM),
                  pl.BlockSpec((step_idx,), lambda i: (jnp.minimum(i + 1, nsteps - 1),), memory_space=pltpu.SMEM),
                  pl.BlockSpec((PEER_TOKENS, dm), tok_block),
                  pl.BlockSpec((PEER_TOKENS, PEER_PAIRS), tok_block),
                  pl.BlockSpec(memory_space=pl.ANY)],
        out_specs=pl.BlockSpec((PEER_TOKENS, dm), tok_block),
        out_shape=jax.ShapeDtypeStruct((ntok, dm), F32),
        scratch_shapes=[pltpu.VMEM((slab, PEER_PAIRS // SUBLANES, SUBLANES, w), jnp.uint32)] * PEER_TOKENS
                       + [pltpu.SemaphoreType.DMA((PEER_TOKENS,))],
        compiler_params=pltpu.CompilerParams(dimension_semantics=("arbitrary",)),
        name="peer_experts",
    )(idx.reshape(-1), idx.reshape(-1), h, gate, uv)


def rmsnorm(x, g=None):
    y = x * lax.rsqrt(jnp.mean(x * x, axis=-1, keepdims=True) + EPS)
    return y if g is None else y * g.astype(F32)


def l2norm(x):
    return x * lax.rsqrt(jnp.sum(x * x, axis=-1, keepdims=True) + EPS)


def adaln(cond, w, b):
    n = cond.shape[0]
    cp = jnp.pad(jax.nn.silu(cond), ((0, (-n) % SUBLANES), (0, 0)))
    m = (matmul(cp, w)[:n] + b)[:, None, :]
    return jnp.split(m, N_MOD, axis=-1)


def dwconv(x, w, b=None):
    pad = SHORT_CONV // 2
    y = lax.conv_general_dilated(x, w[:, None, :].astype(x.dtype), window_strides=(1,),
                                 padding=[(pad, pad)], dimension_numbers=('NWC', 'WIO', 'NWC'),
                                 feature_group_count=x.shape[-1])
    return y if b is None else y + b.astype(x.dtype)


def segment_conv(x, n_ctx, w, b=None):
    return jnp.concatenate([dwconv(x[:, :n_ctx], w, b), dwconv(x[:, n_ctx:], w, b)], axis=1)


def axial_rope_tables(rows, dtype):
    row = jnp.repeat(jnp.arange(rows, dtype=F32), GRID_W)
    col = jnp.tile(jnp.arange(GRID_W, dtype=F32), rows)
    nf = A_DIM // 4
    inv = ROPE_THETA ** (-jnp.arange(nf, dtype=F32) / nf)
    ar, ac = row[:, None] * inv, col[:, None] * inv
    return tuple(t.astype(dtype) for t in (jnp.cos(ar), jnp.sin(ar), jnp.cos(ac), jnp.sin(ac)))


def rope_half(x, cos, sin):
    x1, x2 = jnp.split(x, 2, axis=-1)
    cs, sn = cos[:, None, None, :], sin[:, None, None, :]
    return jnp.concatenate([x1 * cs - x2 * sn, x2 * cs + x1 * sn], axis=-1)


def axial_rope(x, tabs):
    cr, sr, cc, sc = tabs
    xr, xc = jnp.split(x, 2, axis=-1)
    return jnp.concatenate([rope_half(xr, cr, sr), rope_half(xc, cc, sc)], axis=-1)


def diff_attention(qkv, n_ctx, rope, layer, need_ctx, qn_g, kn_g, lq1, lk1, lq2, lk2, sub_g):
    lam_init = 0.8 - 0.6 * math.exp(-0.3 * layer)
    f = lambda t: t.astype(F32)
    lam = jnp.exp(jnp.sum(f(lq1) * f(lk1))) - jnp.exp(jnp.sum(f(lq2) * f(lk2))) + lam_init
    q, k, v = jnp.split(qkv, 3, axis=-1)
    bn, tn, _ = q.shape

    def norm_rotate(t, g):
        t = rmsnorm(t.reshape(bn, tn, A_HEADS, 2, A_DIM), g)
        return jnp.concatenate([t[:, :n_ctx], axial_rope(t[:, n_ctx:], rope)], axis=1).reshape(bn, tn, -1)

    q = (norm_rotate(q, qn_g) * (A_DIM ** -0.5)).astype(BF16)
    kt = jnp.swapaxes(norm_rotate(k, kn_g).astype(BF16), 1, 2)
    v = v.astype(BF16)
    scale = 1.0 - lam_init
    yl = diff_attention_core(q[:, n_ctx:], kt, v, lam, sub_g, scale)
    if need_ctx:
        yc = diff_attention_core(q[:, :n_ctx], kt[:, :, :n_ctx], v[:, :n_ctx], lam, sub_g, scale)
    else:
        yc = jnp.zeros((bn, n_ctx, yl.shape[-1]), F32)
    return jnp.concatenate([yc, yl], axis=1)


def gated_deltanet(qkv, z, b_raw, a_raw, n_ctx, conv_w, a_log, dt_bias, norm_g):
    bn, tn, _ = qkv.shape
    qkv = jax.nn.silu(segment_conv(qkv, n_ctx, conv_w))
    q, k, v = jnp.split(qkv, 3, axis=-1)
    heads = lambda t: t.reshape(bn, tn, B_HEADS, B_DIM)
    q = (l2norm(heads(q)) * (B_DIM ** -0.5)).reshape(bn, tn, -1)
    k = l2norm(heads(k)).reshape(bn, tn, -1)
    beta = jax.nn.sigmoid(b_raw)
    g = -jnp.exp(a_log.astype(F32)).reshape(-1) * jax.nn.softplus(a_raw + dt_bias.astype(F32).reshape(-1))
    o = (gdn_scan(q, k, v, g[..., :B_HEADS], beta[..., :B_HEADS], n_ctx, rev=False)
         + gdn_scan(q, k, v, g[..., B_HEADS:], beta[..., B_HEADS:], n_ctx, rev=True))
    y = rmsnorm(heads(o), norm_g) * jax.nn.silu(heads(z))
    return y.reshape(bn, tn, -1)


def token_mixer(h, n_ctx, rope, layer, need_ctx, w_in,
                attn_qn_g, attn_kn_g, lam_q1, lam_k1, lam_q2, lam_k2, attn_sub_g,
                gdn_conv_w, gdn_a_log, gdn_dt_bias, gdn_norm_g,
                lru_conv_w, lru_conv_b, lru_w_r, lru_b_r, lru_w_i, lru_b_i, lru_lambda,
                w_branch, w_out):
    offs = np.cumsum((0,) + IN_SPLITS)
    big = [0, 1, 2, 5, 6, 7]
    w_big = jnp.concatenate([w_in[:, offs[i]:offs[i + 1]] for i in big], axis=1).astype(BF16)
    w_small = jnp.pad(w_in[:, offs[3]:offs[5]], ((0, 0), (0, LANES - 4 * B_HEADS))).astype(BF16)
    big_offs = np.cumsum([IN_SPLITS[i] for i in big])[:-1].tolist()
    p_attn, p_gdn, p_z, p_lru, p_lru_gate, p_merge = jnp.split(matmul3(h, w_big), big_offs, axis=-1)
    p_small = matmul3(h, w_small, tn=LANES)
    b_raw, a_raw = p_small[..., :2 * B_HEADS], p_small[..., 2 * B_HEADS:4 * B_HEADS]

    ya = diff_attention(p_attn, n_ctx, rope, layer, need_ctx, attn_qn_g, attn_kn_g,
                        lam_q1, lam_k1, lam_q2, lam_k2, attn_sub_g)
    yb = gated_deltanet(p_gdn, p_z, b_raw, a_raw, n_ctx, gdn_conv_w, gdn_a_log, gdn_dt_bias, gdn_norm_g)
    yr = lru_scan(segment_conv(p_lru, n_ctx, lru_conv_w, lru_conv_b), p_lru_gate,
                  lru_w_r, lru_b_r, lru_w_i, lru_b_i, lru_lambda, n_ctx)
    wb = w_branch.astype(BF16)
    gates = jnp.split(jax.nn.sigmoid(p_merge), N_BRANCH, axis=-1)
    y = gates[0] * matmul3(ya, wb[0])
    for j, yj in ((1, yb), (2, yr)):
        y = y + gates[j] * matmul3(yj, wb[j])
    return matmul3(y, w_out.astype(BF16))


def peer_ffn(h, wq, subkeys, u, v):
    idx, gate = peer_route(matmul(h, wq.astype(BF16)), subkeys)
    return peer_experts(h, idx, gate, u, v)


def kernel(x, c, ctx, c_ctx, w_ada, b_ada, norm1_g, norm2_g, w_in, attn_qn_g, attn_kn_g, lam_q1, lam_k1, lam_q2, lam_k2, attn_sub_g, gdn_conv_w, gdn_a_log, gdn_dt_bias, gdn_norm_g, lru_conv_w, lru_conv_b, lru_w_r, lru_b_r, lru_w_i, lru_b_i, lru_lambda, w_branch, w_out, peer_wq, peer_subkeys, peer_u, peer_v):
    bn, n_lat, dm = x.shape
    n_ctx = ctx.shape[1]
    rope = axial_rope_tables(n_lat // GRID_W, x.dtype)
    is_ctx = (jnp.arange(n_ctx + n_lat) < n_ctx)[None, :, None]
    xa = jnp.concatenate([ctx, x], axis=1)
    for l in range(DEPTH):
        need_ctx = l < DEPTH - 1
        w_ada_l = w_ada[l].astype(BF16)
        ml = adaln(c, w_ada_l, b_ada[l])
        mc = adaln(c_ctx[None, :], w_ada_l, b_ada[l])
        mod = [jnp.where(is_ctx, mc[j], ml[j]) for j in range(N_MOD)]
        h = rmsnorm(xa, norm1_g[l]) * (1.0 + mod[1]) + mod[0]
        y = token_mixer(h, n_ctx, rope, l, need_ctx, w_in[l],
                        attn_qn_g[l], attn_kn_g[l], lam_q1[l], lam_k1[l], lam_q2[l], lam_k2[l], attn_sub_g[l],
                        gdn_conv_w[l], gdn_a_log[l], gdn_dt_bias[l], gdn_norm_g[l],
                        lru_conv_w[l], lru_conv_b[l], lru_w_r[l], lru_b_r[l], lru_w_i[l], lru_b_i[l], lru_lambda[l],
                        w_branch[l], w_out[l])
        xa = xa + mod[2] * y
        if not need_ctx:
            xa, mod = xa[:, n_ctx:], [ml[j] for j in range(N_MOD)]
        h = rmsnorm(xa, norm2_g[l]) * (1.0 + mod[4]) + mod[3]
        f = peer_ffn(h.reshape(-1, dm), peer_wq[l], peer_subkeys[l], peer_u[l], peer_v[l])
        xa = xa + mod[5] * f.reshape(xa.shape)
    return xa
```

```python
import functools
import math

import jax
import jax.numpy as jnp
import numpy as np
from jax import lax
from jax.experimental import pallas as pl
from jax.experimental.pallas import tpu as pltpu

D_MODEL = 1024
DEPTH = 4
GRID_W = 64
EPS = 1e-6
N_MOD = 6
A_HEADS = 8
A_DIM = 64
ROPE_THETA = 10000.0
B_HEADS = 8
B_DIM = 128
B_CHUNK = 64
SHORT_CONV = 5
C_WIDTH = 1024
C_BLOCKS = 16
LRU_C = 8.0
N_BRANCH = 3
P_HEADS = 8
P_KEYS = 128
P_QDIM = 256
P_TOPK = 16

A_QKV = 3 * A_HEADS * 2 * A_DIM
B_QKV = 3 * B_HEADS * B_DIM
IN_SPLITS = (A_QKV, B_QKV, B_HEADS * B_DIM, 2 * B_HEADS, 2 * B_HEADS, C_WIDTH, C_WIDTH, N_BRANCH * D_MODEL)

F32 = jnp.float32
BF16 = jnp.bfloat16
HIGHEST = lax.Precision.HIGHEST
NT_DIMS = (((1,), (1,)), ((), ()))
SUBLANES = 8
LANES = 128


def _matmul_kernel(x_ref, w_ref, o_ref):
    o_ref[...] = jnp.dot(x_ref[...].astype(BF16), w_ref[...].astype(BF16),
                         preferred_element_type=F32).astype(o_ref.dtype)


def matmul(x, w, *, tm=1024, tn=1024, out_dtype=F32):
    m, k = x.shape
    n = w.shape[1]
    tm, tn = min(tm, m), min(tn, n)
    assert m % tm == 0 and n % tn == 0, (m, n, tm, tn)
    return pl.pallas_call(
        _matmul_kernel,
        grid=(m // tm, n // tn),
        in_specs=[pl.BlockSpec((tm, k), lambda i, j: (i, 0)),
                  pl.BlockSpec((k, tn), lambda i, j: (0, j))],
        out_specs=pl.BlockSpec((tm, tn), lambda i, j: (i, j)),
        out_shape=jax.ShapeDtypeStruct((m, n), out_dtype),
        compiler_params=pltpu.CompilerParams(dimension_semantics=("parallel", "parallel")),
        name="matmul",
    )(x, w)


def matmul3(x, w, **kw):
    b, t, k = x.shape
    return matmul(x.reshape(b * t, k), w, **kw).reshape(b, t, -1)


def _diff_attn_kernel(lam_ref, q_ref, kt_ref, v_ref, g_ref, o_ref, *, out_scale):
    q = q_ref[0]
    kt = kt_ref[0]
    v = v_ref[0]
    lane = lax.broadcasted_iota(jnp.int32, q.shape, 1)
    zero = jnp.zeros_like(q)

    def softmax_v(qc):
        s = jnp.dot(qc, kt, preferred_element_type=F32)
        p = jnp.exp(s - jnp.max(s, axis=-1, keepdims=True))
        l = jnp.sum(p, axis=-1, keepdims=True)
        return jnp.dot(p.astype(BF16), v, preferred_element_type=F32) / l

    o = softmax_v(jnp.where(lane < A_DIM, q, zero)) - lam_ref[0] * softmax_v(jnp.where(lane >= A_DIM, q, zero))
    y = o * lax.rsqrt(jnp.mean(o * o, axis=-1, keepdims=True) + EPS) * g_ref[...]
    o_ref[0] = y * out_scale


def diff_attention_core(q, kt, v, lam, sub_g, out_scale, *, tq=256):
    bn, tq_all, _ = q.shape
    tk = v.shape[1]
    tq = min(tq, tq_all)
    hd = 2 * A_DIM
    return pl.pallas_call(
        functools.partial(_diff_attn_kernel, out_scale=out_scale),
        grid=(bn, A_HEADS, tq_all // tq),
        in_specs=[pl.BlockSpec(memory_space=pltpu.SMEM),
                  pl.BlockSpec((1, tq, hd), lambda b, h, i: (b, i, h)),
                  pl.BlockSpec((1, hd, tk), lambda b, h, i: (b, h, 0)),
                  pl.BlockSpec((1, tk, hd), lambda b, h, i: (b, 0, h)),
                  pl.BlockSpec((1, hd), lambda b, h, i: (0, 0))],
        out_specs=pl.BlockSpec((1, tq, hd), lambda b, h, i: (b, i, h)),
        out_shape=jax.ShapeDtypeStruct((bn, tq_all, A_HEADS * hd), F32),
        compiler_params=pltpu.CompilerParams(dimension_semantics=("parallel", "parallel", "parallel")),
        name="diff_attn",
    )(lam.reshape(1).astype(F32), q, kt, v, sub_g.reshape(1, hd).astype(F32))


BMM = (((2,), (1,)), ((0,), (0,)))
BMM_NT = (((2,), (2,)), ((0,), (0,)))
BMM_TN = (((1,), (1,)), ((0,), (0,)))


def _bdot(a, b, dims=BMM):
    return lax.dot_general(a, b, dims, preferred_element_type=F32)


def _bdot3(a, b):
    a_hi, b_hi = a.astype(BF16), b.astype(BF16)
    a_lo = (a - a_hi.astype(F32)).astype(BF16)
    b_lo = (b - b_hi.astype(F32)).astype(BF16)
    return _bdot(a_hi, b_hi) + (_bdot(a_hi, b_lo) + _bdot(a_lo, b_hi))


def _gdn_kernel(q_ref, k_ref, v_ref, gcol_ref, bcol_ref, grow_ref, o_ref, s_ref, *, rev):
    @pl.when(pl.program_id(1) == 0)
    def _():
        s_ref[...] = jnp.zeros_like(s_ref)

    c = B_CHUNK
    ii = lax.broadcasted_iota(jnp.int32, (c, c), 0)
    jj = lax.broadcasted_iota(jnp.int32, (c, c), 1)
    incl = (ii <= jj) if rev else (ii >= jj)
    strict = (ii < jj) if rev else (ii > jj)
    eye = (ii == jj).astype(F32)
    m_incl = incl.astype(F32)
    last = 0 if rev else c - 1
    hs = range(B_HEADS)

    gc_cols = jnp.dot(m_incl, gcol_ref[0], precision=HIGHEST, preferred_element_type=F32)
    gc_rows = lax.dot_general(grow_ref[0, 0], m_incl, NT_DIMS, precision=HIGHEST, preferred_element_type=F32)
    beta_cols = bcol_ref[0]
    gi = jnp.stack([gc_cols[:, h:h + 1] for h in hs])
    gj = jnp.stack([gc_rows[h:h + 1, :] for h in hs])
    beta = jnp.stack([beta_cols[:, h:h + 1] for h in hs])
    heads = lambda ref: jnp.stack([ref[0, :, h * B_DIM:(h + 1) * B_DIM] for h in hs])
    q, k, v = heads(q_ref), heads(k_ref), heads(v_ref)

    dec = jnp.where(incl, jnp.exp(jnp.where(incl, gi - gj, 0.0)), 0.0)
    kb = k * beta
    kk = _bdot(kb, k, BMM_NT)
    qk = _bdot(q, k, BMM_NT) * dec
    lmat = jnp.where(strict, kk * dec, 0.0)
    same = lambda b: (ii // b) == (jj // b)
    base = 4
    nl = -jnp.where(same(base), lmat, 0.0)
    inv = eye + nl
    inv = inv + _bdot3(inv, _bdot3(nl, nl))
    b = base
    while b < c:
        join = jnp.where(same(2 * b) & jnp.logical_not(same(b)), lmat, 0.0)
        inv = inv - _bdot3(_bdot3(inv, join), inv)
        b *= 2
    eg = jnp.exp(gi)
    uw = _bdot3(inv, jnp.concatenate([v * beta, kb * eg], axis=2))
    u, w = uw[:, :, :B_DIM], uw[:, :, B_DIM:]
    s = s_ref[...]
    v_new = u - _bdot(w, s)
    o = _bdot(q * eg, s) + _bdot(qk, v_new)
    for h in hs:
        o_ref[0, :, h * B_DIM:(h + 1) * B_DIM] = o[h]
    g_last = gi[:, last:last + 1, :]
    kd = k * jnp.exp(g_last - gi)
    s_ref[...] = s * jnp.exp(g_last) + _bdot(kd, v_new, BMM_TN)


def gdn_scan(q, k, v, g, beta, n_ctx, *, rev):
    bn, tn, _ = q.shape
    nc = tn // B_CHUNK
    ncc = n_ctx // B_CHUNK
    g_rows = jnp.swapaxes(g.reshape(bn, nc, B_CHUNK, B_HEADS), 2, 3)

    def chunk(n):
        return jnp.where(n < ncc, ncc - 1 - n, nc - 1 + ncc - n) if rev else n

    tok = lambda b, n: (b, chunk(n), 0)
    wide = pl.BlockSpec((1, B_CHUNK, B_HEADS * B_DIM), tok)
    narrow = pl.BlockSpec((1, B_CHUNK, B_HEADS), tok)
    return pl.pallas_call(
        functools.partial(_gdn_kernel, rev=rev),
        grid=(bn, nc),
        in_specs=[wide, wide, wide, narrow, narrow,
                  pl.BlockSpec((1, 1, B_HEADS, B_CHUNK), lambda b, n: (b, chunk(n), 0, 0))],
        out_specs=wide,
        out_shape=jax.ShapeDtypeStruct((bn, tn, B_HEADS * B_DIM), F32),
        scratch_shapes=[pltpu.VMEM((B_HEADS, B_DIM, B_DIM), F32)],
        compiler_params=pltpu.CompilerParams(dimension_semantics=("parallel", "arbitrary")),
        name="gdn_scan",
    )(q, k, v, g, beta, g_rows)


LRU_CG = 256
LRU_ROWS = 256


def _lru_tile_scan(a, b, h, row, rev):
    for s in (1, 2, 4):
        shift = SUBLANES - s if rev else s
        valid = (row < SUBLANES - s) if rev else (row >= s)
        a_s = pltpu.roll(a, shift, 0)
        b_s = pltpu.roll(b, shift, 0)
        b = jnp.where(valid, a * b_s + b, b)
        a = jnp.where(valid, a * a_s, a)
    ht = a * h + b
    return ht, (ht[0:1] if rev else ht[SUBLANES - 1:SUBLANES])


def _lru_kernel(x_ref, gate_ref, wr_ref, wi_ref, br_ref, bi_ref, c_ref, o_ref, *, n_ctx):
    tn = x_ref.shape[1]
    row = lax.broadcasted_iota(jnp.int32, (SUBLANES, LRU_CG), 0)
    ntile = LRU_ROWS // SUBLANES

    def chunk(r0, h, d, rev):
        xs = x_ref[0, pl.ds(r0, LRU_ROWS), :]
        xb = xs.astype(BF16)
        r = jax.nn.sigmoid(jnp.dot(xb, wr_ref[d, 0], preferred_element_type=F32) + br_ref[d])
        i = jax.nn.sigmoid(jnp.dot(xb, wi_ref[d, 0], preferred_element_type=F32) + bi_ref[d])
        log_a = c_ref[d] * r
        a = jnp.exp(log_a)
        b = jnp.sqrt(jnp.maximum(1.0 - a * a, 0.0)) * (i * xs)
        tiles = [None] * ntile
        for ti in (reversed(range(ntile)) if rev else range(ntile)):
            sl = slice(ti * SUBLANES, (ti + 1) * SUBLANES)
            tiles[ti], h = _lru_tile_scan(a[sl], b[sl], h, row, rev)
        hs = jnp.concatenate(tiles, axis=0)
        if d == 0:
            o_ref[0, pl.ds(r0, LRU_ROWS), :] = hs
        else:
            gate = gate_ref[0, pl.ds(r0, LRU_ROWS), :]
            o_ref[0, pl.ds(r0, LRU_ROWS), :] = (o_ref[0, pl.ds(r0, LRU_ROWS), :] + hs) * jax.nn.gelu(gate)
        return h

    for d, rev in ((0, False), (1, True)):
        h = jnp.zeros((1, LRU_CG), F32)
        for s0, s1 in ((0, n_ctx), (n_ctx, tn)):
            nch = (s1 - s0) // LRU_ROWS

            def body(ci, h, s0=s0, nch=nch, d=d, rev=rev):
                cidx = nch - 1 - ci if rev else ci
                return chunk(pl.multiple_of(s0 + cidx * LRU_ROWS, LRU_ROWS), h, d, rev)

            h = lax.fori_loop(0, nch, body, h)


def lru_scan(x, gate, w_r, b_r, w_i, b_i, lam, n_ctx):
    bn, tn, cw = x.shape
    assert n_ctx % LRU_ROWS == 0 and (tn - n_ctx) % LRU_ROWS == 0 and cw % LRU_CG == 0
    ncg = cw // LRU_CG
    per = LRU_CG // (C_WIDTH // C_BLOCKS)

    def blockdiag_tiles(w):
        blk = w.shape[-1]
        wt = w.astype(F32).reshape(2, ncg, per, blk, blk)
        eye = jnp.eye(per, dtype=F32)
        return jnp.einsum('dgpab,pq->dgpaqb', wt, eye).reshape(2, ncg, LRU_CG, LRU_CG).astype(BF16)

    cvec = (-LRU_C * jax.nn.softplus(-lam.astype(F32))).reshape(2, 1, cw)
    tok = pl.BlockSpec((1, tn, LRU_CG), lambda b, g: (b, 0, g))
    wspec = pl.BlockSpec((2, 1, LRU_CG, LRU_CG), lambda b, g: (0, g, 0, 0))
    vspec = pl.BlockSpec((2, 1, LRU_CG), lambda b, g: (0, 0, g))
    return pl.pallas_call(
        functools.partial(_lru_kernel, n_ctx=n_ctx),
        grid=(bn, ncg),
        in_specs=[tok, tok, wspec, wspec, vspec, vspec, vspec],
        out_specs=tok,
        out_shape=jax.ShapeDtypeStruct((bn, tn, cw), F32),
        compiler_params=pltpu.CompilerParams(dimension_semantics=("parallel", "parallel")),
        name="lru_scan",
    )(x, gate, blockdiag_tiles(w_r), blockdiag_tiles(w_i),
      b_r.astype(F32).reshape(2, 1, cw), b_i.astype(F32).reshape(2, 1, cw), cvec)


ROUTE_TOKENS = 128
PEER_PAIRS = P_HEADS * P_TOPK


def _sublane_topk(s, k, payload=None):
    g, width, tb = s.shape
    row = lax.broadcasted_iota(jnp.int32, s.shape, 1).astype(F32)
    out_row = lax.broadcasted_iota(jnp.int32, (g, k, tb), 1)
    vals = jnp.zeros((g, k, tb), F32)
    picked = jnp.zeros((g, k, tb), F32)
    for r in range(k):
        m = jnp.max(s, axis=1, keepdims=True)
        p = jnp.min(jnp.where(s == m, row, float(width)), axis=1, keepdims=True)
        sel = row == p
        if payload is not None:
            p = jnp.sum(jnp.where(sel, payload, 0.0), axis=1, keepdims=True)
        vals = jnp.where(out_row == r, m, vals)
        picked = jnp.where(out_row == r, p, picked)
        s = jnp.where(sel, -jnp.inf, s)
    return vals, picked


def _peer_route_kernel(q_ref, sk_ref, idx_ref, gate_ref):
    half = P_QDIM // 2

    def scores(c):
        out = []
        for h in range(P_HEADS):
            x = q_ref[:, (2 * h + c) * half:(2 * h + c + 1) * half]
            xn = x * lax.rsqrt(jnp.mean(x * x, axis=-1, keepdims=True) + EPS)
            out.append(lax.dot_general(sk_ref[h, c], xn, NT_DIMS, preferred_element_type=F32))
        return jnp.stack(out)

    v1, i1 = _sublane_topk(scores(0), P_TOPK)
    v2, i2 = _sublane_topk(scores(1), P_TOPK)
    cand = jnp.concatenate([v1[:, a:a + 1] + v2 for a in range(P_TOPK)], axis=1)
    cand_idx = jnp.concatenate([i1[:, a:a + 1] * float(P_KEYS) + i2 for a in range(P_TOPK)], axis=1)
    best, eid = _sublane_topk(cand, P_TOPK, payload=cand_idx)
    ex = jnp.exp(best - jnp.max(best, axis=1, keepdims=True))
    gate = ex / jnp.sum(ex, axis=1, keepdims=True)
    for h in range(P_HEADS):
        idx_ref[h * P_TOPK:(h + 1) * P_TOPK, :] = eid[h].astype(jnp.int32)
        gate_ref[h * P_TOPK:(h + 1) * P_TOPK, :] = gate[h]


def peer_route(q, subkeys):
    ntok = q.shape[0]
    assert ntok % ROUTE_TOKENS == 0
    out_block = pl.BlockSpec((PEER_PAIRS, ROUTE_TOKENS), lambda i: (0, i))
    idx_t, gate_t = pl.pallas_call(
        _peer_route_kernel,
        grid=(ntok // ROUTE_TOKENS,),
        in_specs=[pl.BlockSpec((ROUTE_TOKENS, q.shape[1]), lambda i: (i, 0)),
                  pl.BlockSpec(subkeys.shape, lambda i: (0, 0, 0, 0))],
        out_specs=[out_block, out_block],
        out_shape=[jax.ShapeDtypeStruct((PEER_PAIRS, ntok), jnp.int32),
                   jax.ShapeDtypeStruct((PEER_PAIRS, ntok), F32)],
        compiler_params=pltpu.CompilerParams(dimension_semantics=("parallel",)),
        name="peer_route",
    )(q, subkeys.astype(F32))
    return idx_t.T, gate_t.T


PEER_TOKENS = 8
PEER_LOOKAHEAD = 4


def _peer_expert_kernel(idx_cur, idx_nxt, h_ref, gate_ref, uv_hbm, o_ref, *scratch, nsteps):
    bufs, sem = scratch[:PEER_TOKENS], scratch[PEER_TOKENS]
    i = pl.program_id(0)
    slab, w = uv_hbm.shape[1:]
    half = slab // 2

    def issue(idx_ref, t, g, j):
        pltpu.make_async_copy(uv_hbm.at[idx_ref[t * PEER_PAIRS + g * SUBLANES + j]], bufs[t].at[:, g, j],
                              sem.at[t]).start(priority=j % 2)

    def wait(t):
        pltpu.make_async_copy(bufs[t], bufs[t], sem.at[t]).wait()

    @pl.when(i == 0)
    def _():
        for t in range(PEER_LOOKAHEAD):
            def body(g, carry, t=t):
                for j in range(SUBLANES):
                    issue(idx_cur, t, g, j)
                return carry
            lax.fori_loop(0, PEER_PAIRS // SUBLANES, body, 0)

    eye_t = (lax.broadcasted_iota(jnp.int32, (PEER_TOKENS, PEER_TOKENS), 0)
             == lax.broadcasted_iota(jnp.int32, (PEER_TOKENS, PEER_TOKENS), 1))
    tn_dims = (((0,), (0,)), ((), ()))
    hb_t = lax.dot_general(h_ref[...].astype(BF16), eye_t.astype(BF16), tn_dims,
                           preferred_element_type=F32).astype(BF16)
    gate_t = lax.dot_general(gate_ref[...], eye_t.astype(F32), tn_dims, precision=HIGHEST,
                             preferred_element_type=F32)
    tok = lax.broadcasted_iota(jnp.int32, gate_t.shape, 1)
    acc = jnp.zeros(o_ref.shape, F32)

    def half_rows(t, first):
        return jnp.concatenate([bufs[t][r].reshape(PEER_PAIRS, w).astype(BF16)
                                for r in range(first, first + half)], axis=1)

    def second_stage(t, s, acc):
        a = jnp.where(tok == t, jax.nn.gelu(s) * gate_t, 0.0)
        return acc + lax.dot_general(a.astype(BF16), half_rows(t, half), tn_dims, preferred_element_type=F32)

    s_prev = None
    for t in range(PEER_TOKENS):
        wait(t)
        ahead = t + PEER_LOOKAHEAD
        for k in range(PEER_PAIRS):
            issue(idx_cur if ahead < PEER_TOKENS else idx_nxt, ahead % PEER_TOKENS, k // SUBLANES, k % SUBLANES)
        s = jnp.dot(half_rows(t, 0), hb_t, preferred_element_type=F32)
        if t:
            acc = second_stage(t - 1, s_prev, acc)
        s_prev = s
    o_ref[...] = second_stage(PEER_TOKENS - 1, s_prev, acc)

    @pl.when(i == nsteps - 1)
    def _():
        for t in range(PEER_LOOKAHEAD):
            wait(t)


def peer_experts(h, idx, gate, u, v):
    ntok, dm = h.shape
    assert ntok % PEER_TOKENS == 0 and idx.shape == (ntok, PEER_PAIRS) and dm % (SUBLANES * LANES) == 0
    nsteps = ntok // PEER_TOKENS
    slab, w = 2 * dm // LANES, LANES
    uv = jnp.concatenate([u, v], axis=1).reshape(u.shape[0], slab, w)
    step_idx = PEER_TOKENS * PEER_PAIRS
    tok_block = lambda i: (i, 0)
    return pl.pallas_call(
        functools.partial(_peer_expert_kernel, nsteps=nsteps),
        grid=(nsteps,),
        in_specs=[pl.BlockSpec((step_idx,), lambda i: (i,), memory_space=pltpu.SMEM),
                  pl.BlockSpec((step_idx,), lambda i: (jnp.minimum(i + 1, nsteps - 1),), memory_space=pltpu.SMEM),
                  pl.BlockSpec((PEER_TOKENS, dm), tok_block),
                  pl.BlockSpec((PEER_TOKENS, PEER_PAIRS), tok_block),
                  pl.BlockSpec(memory_space=pl.ANY)],
        out_specs=pl.BlockSpec((PEER_TOKENS, dm), tok_block),
        out_shape=jax.ShapeDtypeStruct((ntok, dm), F32),
        scratch_shapes=[pltpu.VMEM((slab, PEER_PAIRS // SUBLANES, SUBLANES, w), F32)] * PEER_TOKENS
                       + [pltpu.SemaphoreType.DMA((PEER_TOKENS,))],
        compiler_params=pltpu.CompilerParams(dimension_semantics=("arbitrary",)),
        name="peer_experts",
    )(idx.reshape(-1), idx.reshape(-1), h, gate, uv)


def rmsnorm(x, g=None):
    y = x * lax.rsqrt(jnp.mean(x * x, axis=-1, keepdims=True) + EPS)
    return y if g is None else y * g.astype(F32)


def l2norm(x):
    return x * lax.rsqrt(jnp.sum(x * x, axis=-1, keepdims=True) + EPS)


def adaln(cond, w, b):
    n = cond.shape[0]
    cp = jnp.pad(jax.nn.silu(cond), ((0, (-n) % SUBLANES), (0, 0)))
    m = (matmul(cp, w)[:n] + b)[:, None, :]
    return jnp.split(m, N_MOD, axis=-1)


def dwconv(x, w, b=None):
    pad = SHORT_CONV // 2
    y = lax.conv_general_dilated(x, w[:, None, :].astype(x.dtype), window_strides=(1,),
                                 padding=[(pad, pad)], dimension_numbers=('NWC', 'WIO', 'NWC'),
                                 feature_group_count=x.shape[-1])
    return y if b is None else y + b.astype(x.dtype)


def segment_conv(x, n_ctx, w, b=None):
    return jnp.concatenate([dwconv(x[:, :n_ctx], w, b), dwconv(x[:, n_ctx:], w, b)], axis=1)


def axial_rope_tables(rows, dtype):
    row = jnp.repeat(jnp.arange(rows, dtype=F32), GRID_W)
    col = jnp.tile(jnp.arange(GRID_W, dtype=F32), rows)
    nf = A_DIM // 4
    inv = ROPE_THETA ** (-jnp.arange(nf, dtype=F32) / nf)
    ar, ac = row[:, None] * inv, col[:, None] * inv
    return tuple(t.astype(dtype) for t in (jnp.cos(ar), jnp.sin(ar), jnp.cos(ac), jnp.sin(ac)))


def rope_half(x, cos, sin):
    x1, x2 = jnp.split(x, 2, axis=-1)
    cs, sn = cos[:, None, None, :], sin[:, None, None, :]
    return jnp.concatenate([x1 * cs - x2 * sn, x2 * cs + x1 * sn], axis=-1)


def axial_rope(x, tabs):
    cr, sr, cc, sc = tabs
    xr, xc = jnp.split(x, 2, axis=-1)
    return jnp.concatenate([rope_half(xr, cr, sr), rope_half(xc, cc, sc)], axis=-1)


def diff_attention(qkv, n_ctx, rope, layer, need_ctx, qn_g, kn_g, lq1, lk1, lq2, lk2, sub_g):
    lam_init = 0.8 - 0.6 * math.exp(-0.3 * layer)
    f = lambda t: t.astype(F32)
    lam = jnp.exp(jnp.sum(f(lq1) * f(lk1))) - jnp.exp(jnp.sum(f(lq2) * f(lk2))) + lam_init
    q, k, v = jnp.split(qkv, 3, axis=-1)
    bn, tn, _ = q.shape

    def norm_rotate(t, g):
        t = rmsnorm(t.reshape(bn, tn, A_HEADS, 2, A_DIM), g)
        return jnp.concatenate([t[:, :n_ctx], axial_rope(t[:, n_ctx:], rope)], axis=1).reshape(bn, tn, -1)

    q = (norm_rotate(q, qn_g) * (A_DIM ** -0.5)).astype(BF16)
    kt = jnp.swapaxes(norm_rotate(k, kn_g).astype(BF16), 1, 2)
    v = v.astype(BF16)
    scale = 1.0 - lam_init
    yl = diff_attention_core(q[:, n_ctx:], kt, v, lam, sub_g, scale)
    if need_ctx:
        yc = diff_attention_core(q[:, :n_ctx], kt[:, :, :n_ctx], v[:, :n_ctx], lam, sub_g, scale)
    else:
        yc = jnp.zeros((bn, n_ctx, yl.shape[-1]), F32)
    return jnp.concatenate([yc, yl], axis=1)


def gated_deltanet(qkv, z, b_raw, a_raw, n_ctx, conv_w, a_log, dt_bias, norm_g):
    bn, tn, _ = qkv.shape
    qkv = jax.nn.silu(segment_conv(qkv, n_ctx, conv_w))
    q, k, v = jnp.split(qkv, 3, axis=-1)
    heads = lambda t: t.reshape(bn, tn, B_HEADS, B_DIM)
    q = (l2norm(heads(q)) * (B_DIM ** -0.5)).reshape(bn, tn, -1)
    k = l2norm(heads(k)).reshape(bn, tn, -1)
    beta = jax.nn.sigmoid(b_raw)
    g = -jnp.exp(a_log.astype(F32)).reshape(-1) * jax.nn.softplus(a_raw + dt_bias.astype(F32).reshape(-1))
    o = (gdn_scan(q, k, v, g[..., :B_HEADS], beta[..., :B_HEADS], n_ctx, rev=False)
         + gdn_scan(q, k, v, g[..., B_HEADS:], beta[..., B_HEADS:], n_ctx, rev=True))
    y = rmsnorm(heads(o), norm_g) * jax.nn.silu(heads(z))
    return y.reshape(bn, tn, -1)


def token_mixer(h, n_ctx, rope, layer, need_ctx, w_in,
                attn_qn_g, attn_kn_g, lam_q1, lam_k1, lam_q2, lam_k2, attn_sub_g,
                gdn_conv_w, gdn_a_log, gdn_dt_bias, gdn_norm_g,
                lru_conv_w, lru_conv_b, lru_w_r, lru_b_r, lru_w_i, lru_b_i, lru_lambda,
                w_branch, w_out):
    offs = np.cumsum((0,) + IN_SPLITS)
    big = [0, 1, 2, 5, 6, 7]
    w_big = jnp.concatenate([w_in[:, offs[i]:offs[i + 1]] for i in big], axis=1).astype(BF16)
    w_small = jnp.pad(w_in[:, offs[3]:offs[5]], ((0, 0), (0, LANES - 4 * B_HEADS))).astype(BF16)
    big_offs = np.cumsum([IN_SPLITS[i] for i in big])[:-1].tolist()
    p_attn, p_gdn, p_z, p_lru, p_lru_gate, p_merge = jnp.split(matmul3(h, w_big), big_offs, axis=-1)
    p_small = matmul3(h, w_small, tn=LANES)
    b_raw, a_raw = p_small[..., :2 * B_HEADS], p_small[..., 2 * B_HEADS:4 * B_HEADS]

    ya = diff_attention(p_attn, n_ctx, rope, layer, need_ctx, attn_qn_g, attn_kn_g,
                        lam_q1, lam_k1, lam_q2, lam_k2, attn_sub_g)
    yb = gated_deltanet(p_gdn, p_z, b_raw, a_raw, n_ctx, gdn_conv_w, gdn_a_log, gdn_dt_bias, gdn_norm_g)
    yr = lru_scan(segment_conv(p_lru, n_ctx, lru_conv_w, lru_conv_b), p_lru_gate,
                  lru_w_r, lru_b_r, lru_w_i, lru_b_i, lru_lambda, n_ctx)
    wb = w_branch.astype(BF16)
    gates = jnp.split(jax.nn.sigmoid(p_merge), N_BRANCH, axis=-1)
    y = gates[0] * matmul3(ya, wb[0])
    for j, yj in ((1, yb), (2, yr)):
        y = y + gates[j] * matmul3(yj, wb[j])
    return matmul3(y, w_out.astype(BF16))


def peer_ffn(h, wq, subkeys, u, v):
    idx, gate = peer_route(matmul(h, wq.astype(BF16)), subkeys)
    return peer_experts(h, idx, gate, u, v)


def kernel(x, c, ctx, c_ctx, w_ada, b_ada, norm1_g, norm2_g, w_in, attn_qn_g, attn_kn_g, lam_q1, lam_k1, lam_q2, lam_k2, attn_sub_g, gdn_conv_w, gdn_a_log, gdn_dt_bias, gdn_norm_g, lru_conv_w, lru_conv_b, lru_w_r, lru_b_r, lru_w_i, lru_b_i, lru_lambda, w_branch, w_out, peer_wq, peer_subkeys, peer_u, peer_v):
    bn, n_lat, dm = x.shape
    n_ctx = ctx.shape[1]
    rope = axial_rope_tables(n_lat // GRID_W, x.dtype)
    is_ctx = (jnp.arange(n_ctx + n_lat) < n_ctx)[None, :, None]
    xa = jnp.concatenate([ctx, x], axis=1)
    for l in range(DEPTH):
        need_ctx = l < DEPTH - 1
        w_ada_l = w_ada[l].astype(BF16)
        ml = adaln(c, w_ada_l, b_ada[l])
        mc = adaln(c_ctx[None, :], w_ada_l, b_ada[l])
        mod = [jnp.where(is_ctx, mc[j], ml[j]) for j in range(N_MOD)]
        h = rmsnorm(xa, norm1_g[l]) * (1.0 + mod[1]) + mod[0]
        y = token_mixer(h, n_ctx, rope, l, need_ctx, w_in[l],
                        attn_qn_g[l], attn_kn_g[l], lam_q1[l], lam_k1[l], lam_q2[l], lam_k2[l], attn_sub_g[l],
                        gdn_conv_w[l], gdn_a_log[l], gdn_dt_bias[l], gdn_norm_g[l],
                        lru_conv_w[l], lru_conv_b[l], lru_w_r[l], lru_b_r[l], lru_w_i[l], lru_b_i[l], lru_lambda[l],
                        w_branch[l], w_out[l])
        xa = xa + mod[2] * y
        if not need_ctx:
            xa, mod = xa[:, n_ctx:], [ml[j] for j in range(N_MOD)]
        h = rmsnorm(xa, norm2_g[l]) * (1.0 + mod[4]) + mod[3]
        f = peer_ffn(h.reshape(-1, dm), peer_wq[l], peer_subkeys[l], peer_u[l], peer_v[l])
        xa = xa + mod[5] * f.reshape(xa.shape)
    return xa
```

```python
import functools
import math

import jax
import jax.numpy as jnp
import numpy as np
from jax import lax
from jax.experimental import pallas as pl
from jax.experimental.pallas import tpu as pltpu

D_MODEL = 1024
DEPTH = 4
GRID_W = 64
EPS = 1e-6
N_MOD = 6
A_HEADS = 8
A_DIM = 64
ROPE_THETA = 10000.0
B_HEADS = 8
B_DIM = 128
B_CHUNK = 64
SHORT_CONV = 5
C_WIDTH = 1024
C_BLOCKS = 16
LRU_C = 8.0
N_BRANCH = 3
P_HEADS = 8
P_KEYS = 128
P_QDIM = 256
P_TOPK = 16

A_QKV = 3 * A_HEADS * 2 * A_DIM
B_QKV = 3 * B_HEADS * B_DIM
IN_SPLITS = (A_QKV, B_QKV, B_HEADS * B_DIM, 2 * B_HEADS, 2 * B_HEADS, C_WIDTH, C_WIDTH, N_BRANCH * D_MODEL)

F32 = jnp.float32
BF16 = jnp.bfloat16
HIGHEST = lax.Precision.HIGHEST
NT_DIMS = (((1,), (1,)), ((), ()))
SUBLANES = 8
LANES = 128


def _matmul_kernel(x_ref, w_ref, o_ref):
    o_ref[...] = jnp.dot(x_ref[...].astype(BF16), w_ref[...].astype(BF16),
                         preferred_element_type=F32).astype(o_ref.dtype)


def matmul(x, w, *, tm=1024, tn=1024, out_dtype=F32):
    m, k = x.shape
    n = w.shape[1]
    tm, tn = min(tm, m), min(tn, n)
    assert m % tm == 0 and n % tn == 0, (m, n, tm, tn)
    return pl.pallas_call(
        _matmul_kernel,
        grid=(m // tm, n // tn),
        in_specs=[pl.BlockSpec((tm, k), lambda i, j: (i, 0)),
                  pl.BlockSpec((k, tn), lambda i, j: (0, j))],
        out_specs=pl.BlockSpec((tm, tn), lambda i, j: (i, j)),
        out_shape=jax.ShapeDtypeStruct((m, n), out_dtype),
        compiler_params=pltpu.CompilerParams(dimension_semantics=("parallel", "parallel")),
        name="matmul",
    )(x, w)


def matmul3(x, w, **kw):
    b, t, k = x.shape
    return matmul(x.reshape(b * t, k), w, **kw).reshape(b, t, -1)


def _diff_attn_kernel(lam_ref, q_ref, kt_ref, v_ref, g_ref, o_ref, *, out_scale):
    q = q_ref[0]
    kt = kt_ref[0]
    v = v_ref[0]
    lane = lax.broadcasted_iota(jnp.int32, q.shape, 1)
    zero = jnp.zeros_like(q)

    def softmax_v(qc):
        s = jnp.dot(qc, kt, preferred_element_type=F32)
        p = jnp.exp2(s - jnp.max(s, axis=-1, keepdims=True))
        l = jnp.sum(p, axis=-1, keepdims=True)
        return jnp.dot(p.astype(BF16), v, preferred_element_type=F32) / l

    o = softmax_v(jnp.where(lane < A_DIM, q, zero)) - lam_ref[0] * softmax_v(jnp.where(lane >= A_DIM, q, zero))
    y = o * lax.rsqrt(jnp.mean(o * o, axis=-1, keepdims=True) + EPS) * g_ref[...]
    o_ref[0] = y * out_scale


def diff_attention_core(q, kt, v, lam, sub_g, out_scale, *, tq=256):
    bn, tq_all, _ = q.shape
    tk = v.shape[1]
    tq = min(tq, tq_all)
    hd = 2 * A_DIM
    return pl.pallas_call(
        functools.partial(_diff_attn_kernel, out_scale=out_scale),
        grid=(bn, A_HEADS, tq_all // tq),
        in_specs=[pl.BlockSpec(memory_space=pltpu.SMEM),
                  pl.BlockSpec((1, tq, hd), lambda b, h, i: (b, i, h)),
                  pl.BlockSpec((1, hd, tk), lambda b, h, i: (b, h, 0)),
                  pl.BlockSpec((1, tk, hd), lambda b, h, i: (b, 0, h)),
                  pl.BlockSpec((1, hd), lambda b, h, i: (0, 0))],
        out_specs=pl.BlockSpec((1, tq, hd), lambda b, h, i: (b, i, h)),
        out_shape=jax.ShapeDtypeStruct((bn, tq_all, A_HEADS * hd), F32),
        compiler_params=pltpu.CompilerParams(dimension_semantics=("parallel", "parallel", "parallel")),
        name="diff_attn",
    )(lam.reshape(1).astype(F32), q, kt, v, sub_g.reshape(1, hd).astype(F32))


BMM = (((2,), (1,)), ((0,), (0,)))
BMM_NT = (((2,), (2,)), ((0,), (0,)))
BMM_TN = (((1,), (1,)), ((0,), (0,)))


def _bdot(a, b, dims=BMM):
    return lax.dot_general(a, b, dims, preferred_element_type=F32)


def _bdot3(a, b):
    a_hi, b_hi = a.astype(BF16), b.astype(BF16)
    a_lo = (a - a_hi.astype(F32)).astype(BF16)
    b_lo = (b - b_hi.astype(F32)).astype(BF16)
    return _bdot(a_hi, b_hi) + (_bdot(a_hi, b_lo) + _bdot(a_lo, b_hi))


def _gdn_kernel(q_ref, k_ref, v_ref, gcol_ref, bcol_ref, grow_ref, o_ref, s_ref, *, rev):
    @pl.when(pl.program_id(1) == 0)
    def _():
        s_ref[...] = jnp.zeros_like(s_ref)

    c = B_CHUNK
    ii = lax.broadcasted_iota(jnp.int32, (c, c), 0)
    jj = lax.broadcasted_iota(jnp.int32, (c, c), 1)
    incl = (ii <= jj) if rev else (ii >= jj)
    strict = (ii < jj) if rev else (ii > jj)
    eye = (ii == jj).astype(F32)
    m_incl = incl.astype(F32)
    last = 0 if rev else c - 1
    hs = range(B_HEADS)

    gc_cols = jnp.dot(m_incl, gcol_ref[0], precision=HIGHEST, preferred_element_type=F32)
    gc_rows = lax.dot_general(grow_ref[0, 0], m_incl, NT_DIMS, precision=HIGHEST, preferred_element_type=F32)
    beta_cols = bcol_ref[0]
    gi = jnp.stack([gc_cols[:, h:h + 1] for h in hs])
    gj = jnp.stack([gc_rows[h:h + 1, :] for h in hs])
    beta = jnp.stack([beta_cols[:, h:h + 1] for h in hs])
    heads = lambda ref: jnp.stack([ref[0, :, h * B_DIM:(h + 1) * B_DIM] for h in hs])
    q, k, v = heads(q_ref), heads(k_ref), heads(v_ref)

    dec = jnp.where(incl, jnp.exp(jnp.where(incl, gi - gj, 0.0)), 0.0)
    kb = k * beta
    kk = _bdot(kb, k, BMM_NT)
    qk = _bdot(q, k, BMM_NT) * dec
    lmat = jnp.where(strict, kk * dec, 0.0)
    same = lambda b: (ii // b) == (jj // b)
    base = 4
    nl = -jnp.where(same(base), lmat, 0.0)
    inv = eye + nl
    inv = inv + _bdot3(inv, _bdot3(nl, nl))
    b = base
    while b < c:
        join = jnp.where(same(2 * b) & jnp.logical_not(same(b)), lmat, 0.0)
        inv = inv - _bdot3(_bdot3(inv, join), inv)
        b *= 2
    eg = jnp.exp(gi)
    uw = _bdot(inv, jnp.concatenate([v * beta, kb * eg], axis=2))
    u, w = uw[:, :, :B_DIM], uw[:, :, B_DIM:]
    s = s_ref[...]
    v_new = u - _bdot(w, s)
    o = _bdot(q * eg, s) + _bdot(qk, v_new)
    for h in hs:
        o_ref[0, :, h * B_DIM:(h + 1) * B_DIM] = o[h]
    g_last = gi[:, last:last + 1, :]
    kd = k * jnp.exp(g_last - gi)
    s_ref[...] = s * jnp.exp(g_last) + _bdot(kd, v_new, BMM_TN)


def gdn_scan(q, k, v, g, beta, n_ctx, *, rev):
    bn, tn, _ = q.shape
    nc = tn // B_CHUNK
    ncc = n_ctx // B_CHUNK
    g_rows = jnp.swapaxes(g.reshape(bn, nc, B_CHUNK, B_HEADS), 2, 3)

    def chunk(n):
        return jnp.where(n < ncc, ncc - 1 - n, nc - 1 + ncc - n) if rev else n

    tok = lambda b, n: (b, chunk(n), 0)
    wide = pl.BlockSpec((1, B_CHUNK, B_HEADS * B_DIM), tok)
    narrow = pl.BlockSpec((1, B_CHUNK, B_HEADS), tok)
    return pl.pallas_call(
        functools.partial(_gdn_kernel, rev=rev),
        grid=(bn, nc),
        in_specs=[wide, wide, wide, narrow, narrow,
                  pl.BlockSpec((1, 1, B_HEADS, B_CHUNK), lambda b, n: (b, chunk(n), 0, 0))],
        out_specs=wide,
        out_shape=jax.ShapeDtypeStruct((bn, tn, B_HEADS * B_DIM), F32),
        scratch_shapes=[pltpu.VMEM((B_HEADS, B_DIM, B_DIM), F32)],
        compiler_params=pltpu.CompilerParams(dimension_semantics=("parallel", "arbitrary")),
        name="gdn_scan",
    )(q, k, v, g, beta, g_rows)


LRU_CG = 256
LRU_ROWS = 256


def _lru_tile_scan(a, b, h, row, rev):
    for s in (1, 2, 4):
        shift = SUBLANES - s if rev else s
        valid = (row < SUBLANES - s) if rev else (row >= s)
        a_s = pltpu.roll(a, shift, 0)
        b_s = pltpu.roll(b, shift, 0)
        b = jnp.where(valid, a * b_s + b, b)
        a = jnp.where(valid, a * a_s, a)
    ht = a * h + b
    return ht, (ht[0:1] if rev else ht[SUBLANES - 1:SUBLANES])


def _lru_kernel(x_ref, gate_ref, wr_ref, wi_ref, br_ref, bi_ref, c_ref, o_ref, *, n_ctx):
    tn = x_ref.shape[1]
    row = lax.broadcasted_iota(jnp.int32, (SUBLANES, LRU_CG), 0)
    ntile = LRU_ROWS // SUBLANES

    def chunk(r0, h, d, rev):
        xs = x_ref[0, pl.ds(r0, LRU_ROWS), :]
        xb = xs.astype(BF16)
        r = jax.nn.sigmoid(jnp.dot(xb, wr_ref[d, 0], preferred_element_type=F32) + br_ref[d])
        i = jax.nn.sigmoid(jnp.dot(xb, wi_ref[d, 0], preferred_element_type=F32) + bi_ref[d])
        log_a = c_ref[d] * r
        a = jnp.exp(log_a)
        b = jnp.sqrt(jnp.maximum(1.0 - a * a, 0.0)) * (i * xs)
        tiles = [None] * ntile
        for ti in (reversed(range(ntile)) if rev else range(ntile)):
            sl = slice(ti * SUBLANES, (ti + 1) * SUBLANES)
            tiles[ti], h = _lru_tile_scan(a[sl], b[sl], h, row, rev)
        hs = jnp.concatenate(tiles, axis=0)
        if d == 0:
            o_ref[0, pl.ds(r0, LRU_ROWS), :] = hs
        else:
            gate = gate_ref[0, pl.ds(r0, LRU_ROWS), :]
            o_ref[0, pl.ds(r0, LRU_ROWS), :] = (o_ref[0, pl.ds(r0, LRU_ROWS), :] + hs) * jax.nn.gelu(gate)
        return h

    for d, rev in ((0, False), (1, True)):
        h = jnp.zeros((1, LRU_CG), F32)
        for s0, s1 in ((0, n_ctx), (n_ctx, tn)):
            nch = (s1 - s0) // LRU_ROWS

            def body(ci, h, s0=s0, nch=nch, d=d, rev=rev):
                cidx = nch - 1 - ci if rev else ci
                return chunk(pl.multiple_of(s0 + cidx * LRU_ROWS, LRU_ROWS), h, d, rev)

            h = lax.fori_loop(0, nch, body, h)


def lru_scan(x, gate, w_r, b_r, w_i, b_i, lam, n_ctx):
    bn, tn, cw = x.shape
    assert n_ctx % LRU_ROWS == 0 and (tn - n_ctx) % LRU_ROWS == 0 and cw % LRU_CG == 0
    ncg = cw // LRU_CG
    per = LRU_CG // (C_WIDTH // C_BLOCKS)

    def blockdiag_tiles(w):
        blk = w.shape[-1]
        wt = w.astype(F32).reshape(2, ncg, per, blk, blk)
        eye = jnp.eye(per, dtype=F32)
        return jnp.einsum('dgpab,pq->dgpaqb', wt, eye).reshape(2, ncg, LRU_CG, LRU_CG).astype(BF16)

    cvec = (-LRU_C * jax.nn.softplus(-lam.astype(F32))).reshape(2, 1, cw)
    tok = pl.BlockSpec((1, tn, LRU_CG), lambda b, g: (b, 0, g))
    wspec = pl.BlockSpec((2, 1, LRU_CG, LRU_CG), lambda b, g: (0, g, 0, 0))
    vspec = pl.BlockSpec((2, 1, LRU_CG), lambda b, g: (0, 0, g))
    return pl.pallas_call(
        functools.partial(_lru_kernel, n_ctx=n_ctx),
        grid=(bn, ncg),
        in_specs=[tok, tok, wspec, wspec, vspec, vspec, vspec],
        out_specs=tok,
        out_shape=jax.ShapeDtypeStruct((bn, tn, cw), F32),
        compiler_params=pltpu.CompilerParams(dimension_semantics=("parallel", "parallel")),
        name="lru_scan",
    )(x, gate, blockdiag_tiles(w_r), blockdiag_tiles(w_i),
      b_r.astype(F32).reshape(2, 1, cw), b_i.astype(F32).reshape(2, 1, cw), cvec)


ROUTE_TOKENS = 128
PEER_PAIRS = P_HEADS * P_TOPK


def _sublane_topk(s, k, payload=None):
    g, width, tb = s.shape
    row = lax.broadcasted_iota(jnp.int32, s.shape, 1).astype(F32)
    out_row = lax.broadcasted_iota(jnp.int32, (g, k, tb), 1)
    vals = jnp.zeros((g, k, tb), F32)
    picked = jnp.zeros((g, k, tb), F32)
    for r in range(k):
        m = jnp.max(s, axis=1, keepdims=True)
        p = jnp.min(jnp.where(s == m, row, float(width)), axis=1, keepdims=True)
        sel = row == p
        if payload is not None:
            p = jnp.sum(jnp.where(sel, payload, 0.0), axis=1, keepdims=True)
        vals = jnp.where(out_row == r, m, vals)
        picked = jnp.where(out_row == r, p, picked)
        s = jnp.where(sel, -jnp.inf, s)
    return vals, picked


def _peer_route_kernel(q_ref, sk_ref, idx_ref, gate_ref):
    half = P_QDIM // 2

    def scores(c):
        out = []
        for h in range(P_HEADS):
            x = q_ref[:, (2 * h + c) * half:(2 * h + c + 1) * half]
            xn = x * lax.rsqrt(jnp.mean(x * x, axis=-1, keepdims=True) + EPS)
            out.append(lax.dot_general(sk_ref[h, c], xn, NT_DIMS, preferred_element_type=F32))
        return jnp.stack(out)

    v1, i1 = _sublane_topk(scores(0), P_TOPK)
    v2, i2 = _sublane_topk(scores(1), P_TOPK)
    cand = jnp.concatenate([v1[:, a:a + 1] + v2 for a in range(P_TOPK)], axis=1)
    cand_idx = jnp.concatenate([i1[:, a:a + 1] * float(P_KEYS) + i2 for a in range(P_TOPK)], axis=1)
    best, eid = _sublane_topk(cand, P_TOPK, payload=cand_idx)
    ex = jnp.exp(best - jnp.max(best, axis=1, keepdims=True))
    gate = ex / jnp.sum(ex, axis=1, keepdims=True)
    for h in range(P_HEADS):
        idx_ref[h * P_TOPK:(h + 1) * P_TOPK, :] = eid[h].astype(jnp.int32)
        gate_ref[h * P_TOPK:(h + 1) * P_TOPK, :] = gate[h]


def peer_route(q, subkeys):
    ntok = q.shape[0]
    assert ntok % ROUTE_TOKENS == 0
    out_block = pl.BlockSpec((PEER_PAIRS, ROUTE_TOKENS), lambda i: (0, i))
    idx_t, gate_t = pl.pallas_call(
        _peer_route_kernel,
        grid=(ntok // ROUTE_TOKENS,),
        in_specs=[pl.BlockSpec((ROUTE_TOKENS, q.shape[1]), lambda i: (i, 0)),
                  pl.BlockSpec(subkeys.shape, lambda i: (0, 0, 0, 0))],
        out_specs=[out_block, out_block],
        out_shape=[jax.ShapeDtypeStruct((PEER_PAIRS, ntok), jnp.int32),
                   jax.ShapeDtypeStruct((PEER_PAIRS, ntok), F32)],
        compiler_params=pltpu.CompilerParams(dimension_semantics=("parallel",)),
        name="peer_route",
    )(q, subkeys.astype(F32))
    return idx_t.T, gate_t.T


PEER_TOKENS = 8
PEER_LOOKAHEAD = 6


def _peer_expert_kernel(idx_cur, idx_nxt, h_ref, gate_ref, uv_hbm, o_ref, *scratch, nsteps):
    bufs, sem = scratch[:PEER_TOKENS], scratch[PEER_TOKENS]
    i = pl.program_id(0)
    slab, w = uv_hbm.shape[1:]
    half = slab // 2

    def issue(idx_ref, t, g, j):
        pltpu.make_async_copy(uv_hbm.at[idx_ref[t * PEER_PAIRS + g * SUBLANES + j]], bufs[t].at[:, g, j],
                              sem.at[t]).start(priority=j % 2)

    def wait(t):
        pltpu.make_async_copy(bufs[t], bufs[t], sem.at[t]).wait()

    @pl.when(i == 0)
    def _():
        for t in range(PEER_LOOKAHEAD):
            def body(g, carry, t=t):
                for j in range(SUBLANES):
                    issue(idx_cur, t, g, j)
                return carry
            lax.fori_loop(0, PEER_PAIRS // SUBLANES, body, 0)

    eye_t = (lax.broadcasted_iota(jnp.int32, (PEER_TOKENS, PEER_TOKENS), 0)
             == lax.broadcasted_iota(jnp.int32, (PEER_TOKENS, PEER_TOKENS), 1))
    tn_dims = (((0,), (0,)), ((), ()))
    hb_t = lax.dot_general(h_ref[...].astype(BF16), eye_t.astype(BF16), tn_dims,
                           preferred_element_type=F32).astype(BF16)
    gate_t = lax.dot_general(gate_ref[...], eye_t.astype(F32), tn_dims, precision=HIGHEST,
                             preferred_element_type=F32)
    tok = lax.broadcasted_iota(jnp.int32, gate_t.shape, 1)
    acc = jnp.zeros(o_ref.shape, F32)

    def half_rows(t, first):
        return jnp.concatenate([bufs[t][r].reshape(PEER_PAIRS, w).astype(BF16)
                                for r in range(first, first + half)], axis=1)

    def second_stage(t, s, acc):
        a = jnp.where(tok == t, jax.nn.gelu(s) * gate_t, 0.0)
        return acc + lax.dot_general(a.astype(BF16), half_rows(t, half), tn_dims, preferred_element_type=F32)

    s_prev = None
    for t in range(PEER_TOKENS):
        wait(t)
        ahead = t + PEER_LOOKAHEAD
        for k in range(PEER_PAIRS):
            issue(idx_cur if ahead < PEER_TOKENS else idx_nxt, ahead % PEER_TOKENS, k // SUBLANES, k % SUBLANES)
        s = jnp.dot(half_rows(t, 0), hb_t, preferred_element_type=F32)
        if t:
            acc = second_stage(t - 1, s_prev, acc)
        s_prev = s
    o_ref[...] = second_stage(PEER_TOKENS - 1, s_prev, acc)

    @pl.when(i == nsteps - 1)
    def _():
        for t in range(PEER_LOOKAHEAD):
            wait(t)


def peer_experts(h, idx, gate, u, v):
    ntok, dm = h.shape
    assert ntok % PEER_TOKENS == 0 and idx.shape == (ntok, PEER_PAIRS) and dm % (SUBLANES * LANES) == 0
    nsteps = ntok // PEER_TOKENS
    slab, w = 2 * dm // LANES, LANES
    uv = jnp.concatenate([u, v], axis=1).reshape(u.shape[0], slab, w)
    step_idx = PEER_TOKENS * PEER_PAIRS
    tok_block = lambda i: (i, 0)
    return pl.pallas_call(
        functools.partial(_peer_expert_kernel, nsteps=nsteps),
        grid=(nsteps,),
        in_specs=[pl.BlockSpec((step_idx,), lambda i: (i,), memory_space=pltpu.SMEM),
                  pl.BlockSpec((step_idx,), lambda i: (jnp.minimum(i + 1, nsteps - 1),), memory_space=pltpu.SMEM),
                  pl.BlockSpec((PEER_TOKENS, dm), tok_block),
                  pl.BlockSpec((PEER_TOKENS, PEER_PAIRS), tok_block),
                  pl.BlockSpec(memory_space=pl.ANY)],
        out_specs=pl.BlockSpec((PEER_TOKENS, dm), tok_block),
        out_shape=jax.ShapeDtypeStruct((ntok, dm), F32),
        scratch_shapes=[pltpu.VMEM((slab, PEER_PAIRS // SUBLANES, SUBLANES, w), F32)] * PEER_TOKENS
                       + [pltpu.SemaphoreType.DMA((PEER_TOKENS,))],
        compiler_params=pltpu.CompilerParams(dimension_semantics=("arbitrary",)),
        name="peer_experts",
    )(idx.reshape(-1), idx.reshape(-1), h, gate, uv)


def rmsnorm(x, g=None):
    y = x * lax.rsqrt(jnp.mean(x * x, axis=-1, keepdims=True) + EPS)
    return y if g is None else y * g.astype(F32)


def l2norm(x):
    return x * lax.rsqrt(jnp.sum(x * x, axis=-1, keepdims=True) + EPS)


def adaln(cond, w, b):
    n = cond.shape[0]
    cp = jnp.pad(jax.nn.silu(cond), ((0, (-n) % SUBLANES), (0, 0)))
    m = (matmul(cp, w)[:n] + b)[:, None, :]
    return jnp.split(m, N_MOD, axis=-1)


def dwconv(x, w, b=None):
    pad = SHORT_CONV // 2
    y = lax.conv_general_dilated(x, w[:, None, :].astype(x.dtype), window_strides=(1,),
                                 padding=[(pad, pad)], dimension_numbers=('NWC', 'WIO', 'NWC'),
                                 feature_group_count=x.shape[-1])
    return y if b is None else y + b.astype(x.dtype)


def segment_conv(x, n_ctx, w, b=None):
    return jnp.concatenate([dwconv(x[:, :n_ctx], w, b), dwconv(x[:, n_ctx:], w, b)], axis=1)


def axial_rope_tables(rows, dtype):
    row = jnp.repeat(jnp.arange(rows, dtype=F32), GRID_W)
    col = jnp.tile(jnp.arange(GRID_W, dtype=F32), rows)
    nf = A_DIM // 4
    inv = ROPE_THETA ** (-jnp.arange(nf, dtype=F32) / nf)
    ar, ac = row[:, None] * inv, col[:, None] * inv
    return tuple(t.astype(dtype) for t in (jnp.cos(ar), jnp.sin(ar), jnp.cos(ac), jnp.sin(ac)))


def rope_half(x, cos, sin):
    x1, x2 = jnp.split(x, 2, axis=-1)
    cs, sn = cos[:, None, None, :], sin[:, None, None, :]
    return jnp.concatenate([x1 * cs - x2 * sn, x2 * cs + x1 * sn], axis=-1)


def axial_rope(x, tabs):
    cr, sr, cc, sc = tabs
    xr, xc = jnp.split(x, 2, axis=-1)
    return jnp.concatenate([rope_half(xr, cr, sr), rope_half(xc, cc, sc)], axis=-1)


def diff_attention(qkv, n_ctx, rope, layer, need_ctx, qn_g, kn_g, lq1, lk1, lq2, lk2, sub_g):
    lam_init = 0.8 - 0.6 * math.exp(-0.3 * layer)
    f = lambda t: t.astype(F32)
    lam = jnp.exp(jnp.sum(f(lq1) * f(lk1))) - jnp.exp(jnp.sum(f(lq2) * f(lk2))) + lam_init
    q, k, v = jnp.split(qkv, 3, axis=-1)
    bn, tn, _ = q.shape

    def norm_rotate(t, g):
        t = rmsnorm(t.reshape(bn, tn, A_HEADS, 2, A_DIM), g)
        return jnp.concatenate([t[:, :n_ctx], axial_rope(t[:, n_ctx:], rope)], axis=1).reshape(bn, tn, -1)

    q = (norm_rotate(q, qn_g) * (A_DIM ** -0.5 * math.log2(math.e))).astype(BF16)
    kt = jnp.swapaxes(norm_rotate(k, kn_g).astype(BF16), 1, 2)
    v = v.astype(BF16)
    scale = 1.0 - lam_init
    yl = diff_attention_core(q[:, n_ctx:], kt, v, lam, sub_g, scale)
    if need_ctx:
        yc = diff_attention_core(q[:, :n_ctx], kt[:, :, :n_ctx], v[:, :n_ctx], lam, sub_g, scale)
    else:
        yc = jnp.zeros((bn, n_ctx, yl.shape[-1]), F32)
    return jnp.concatenate([yc, yl], axis=1)


def gated_deltanet(qkv, z, b_raw, a_raw, n_ctx, conv_w, a_log, dt_bias, norm_g):
    bn, tn, _ = qkv.shape
    qkv = jax.nn.silu(segment_conv(qkv, n_ctx, conv_w))
    q, k, v = jnp.split(qkv, 3, axis=-1)
    heads = lambda t: t.reshape(bn, tn, B_HEADS, B_DIM)
    q = (l2norm(heads(q)) * (B_DIM ** -0.5)).reshape(bn, tn, -1)
    k = l2norm(heads(k)).reshape(bn, tn, -1)
    beta = jax.nn.sigmoid(b_raw)
    g = -jnp.exp(a_log.astype(F32)).reshape(-1) * jax.nn.softplus(a_raw + dt_bias.astype(F32).reshape(-1))
    o = (gdn_scan(q, k, v, g[..., :B_HEADS], beta[..., :B_HEADS], n_ctx, rev=False)
         + gdn_scan(q, k, v, g[..., B_HEADS:], beta[..., B_HEADS:], n_ctx, rev=True))
    y = rmsnorm(heads(o), norm_g) * jax.nn.silu(heads(z))
    return y.reshape(bn, tn, -1)


def token_mixer(h, n_ctx, rope, layer, need_ctx, w_in,
                attn_qn_g, attn_kn_g, lam_q1, lam_k1, lam_q2, lam_k2, attn_sub_g,
                gdn_conv_w, gdn_a_log, gdn_dt_bias, gdn_norm_g,
                lru_conv_w, lru_conv_b, lru_w_r, lru_b_r, lru_w_i, lru_b_i, lru_lambda,
                w_branch, w_out):
    offs = np.cumsum((0,) + IN_SPLITS)
    big = [0, 1, 2, 5, 6, 7]
    w_big = jnp.concatenate([w_in[:, offs[i]:offs[i + 1]] for i in big], axis=1).astype(BF16)
    w_small = jnp.pad(w_in[:, offs[3]:offs[5]], ((0, 0), (0, LANES - 4 * B_HEADS))).astype(BF16)
    big_offs = np.cumsum([IN_SPLITS[i] for i in big])[:-1].tolist()
    p_attn, p_gdn, p_z, p_lru, p_lru_gate, p_merge = jnp.split(matmul3(h, w_big), big_offs, axis=-1)
    p_small = matmul3(h, w_small, tn=LANES)
    b_raw, a_raw = p_small[..., :2 * B_HEADS], p_small[..., 2 * B_HEADS:4 * B_HEADS]

    ya = diff_attention(p_attn, n_ctx, rope, layer, need_ctx, attn_qn_g, attn_kn_g,
                        lam_q1, lam_k1, lam_q2, lam_k2, attn_sub_g)
    yb = gated_deltanet(p_gdn, p_z, b_raw, a_raw, n_ctx, gdn_conv_w, gdn_a_log, gdn_dt_bias, gdn_norm_g)
    yr = lru_scan(segment_conv(p_lru, n_ctx, lru_conv_w, lru_conv_b), p_lru_gate,
                  lru_w_r, lru_b_r, lru_w_i, lru_b_i, lru_lambda, n_ctx)
    wb = w_branch.astype(BF16)
    gates = jnp.split(jax.nn.sigmoid(p_merge), N_BRANCH, axis=-1)
    y = gates[0] * matmul3(ya, wb[0])
    for j, yj in ((1, yb), (2, yr)):
        y = y + gates[j] * matmul3(yj, wb[j])
    return matmul3(y, w_out.astype(BF16))


def peer_ffn(h, wq, subkeys, u, v):
    idx, gate = peer_route(matmul(h, wq.astype(BF16)), subkeys)
    return peer_experts(h, idx, gate, u, v)


def kernel(x, c, ctx, c_ctx, w_ada, b_ada, norm1_g, norm2_g, w_in, attn_qn_g, attn_kn_g, lam_q1, lam_k1, lam_q2, lam_k2, attn_sub_g, gdn_conv_w, gdn_a_log, gdn_dt_bias, gdn_norm_g, lru_conv_w, lru_conv_b, lru_w_r, lru_b_r, lru_w_i, lru_b_i, lru_lambda, w_branch, w_out, peer_wq, peer_subkeys, peer_u, peer_v):
    bn, n_lat, dm = x.shape
    n_ctx = ctx.shape[1]
    rope = axial_rope_tables(n_lat // GRID_W, x.dtype)
    is_ctx = (jnp.arange(n_ctx + n_lat) < n_ctx)[None, :, None]
    xa = jnp.concatenate([ctx, x], axis=1)
    for l in range(DEPTH):
        need_ctx = l < DEPTH - 1
        w_ada_l = w_ada[l].astype(BF16)
        ml = adaln(c, w_ada_l, b_ada[l])
        mc = adaln(c_ctx[None, :], w_ada_l, b_ada[l])
        mod = [jnp.where(is_ctx, mc[j], ml[j]) for j in range(N_MOD)]
        h = rmsnorm(xa, norm1_g[l]) * (1.0 + mod[1]) + mod[0]
        y = token_mixer(h, n_ctx, rope, l, need_ctx, w_in[l],
                        attn_qn_g[l], attn_kn_g[l], lam_q1[l], lam_k1[l], lam_q2[l], lam_k2[l], attn_sub_g[l],
                        gdn_conv_w[l], gdn_a_log[l], gdn_dt_bias[l], gdn_norm_g[l],
                        lru_conv_w[l], lru_conv_b[l], lru_w_r[l], lru_b_r[l], lru_w_i[l], lru_b_i[l], lru_lambda[l],
                        w_branch[l], w_out[l])
        xa = xa + mod[2] * y
        if not need_ctx:
            xa, mod = xa[:, n_ctx:], [ml[j] for j in range(N_MOD)]
        h = rmsnorm(xa, norm2_g[l]) * (1.0 + mod[4]) + mod[3]
        f = peer_ffn(h.reshape(-1, dm), peer_wq[l], peer_subkeys[l], peer_u[l], peer_v[l])
        xa = xa + mod[5] * f.reshape(xa.shape)
    return xa
```

```python
import functools
import math

import jax
import jax.numpy as jnp
import numpy as np
from jax import lax
from jax.experimental import pallas as pl
from jax.experimental.pallas import tpu as pltpu

D_MODEL = 1024
DEPTH = 4
GRID_W = 64
EPS = 1e-6
N_MOD = 6
A_HEADS = 8
A_DIM = 64
ROPE_THETA = 10000.0
B_HEADS = 8
B_DIM = 128
B_CHUNK = 64
SHORT_CONV = 5
C_WIDTH = 1024
C_BLOCKS = 16
LRU_C = 8.0
N_BRANCH = 3
P_HEADS = 8
P_KEYS = 128
P_QDIM = 256
P_TOPK = 16

A_QKV = 3 * A_HEADS * 2 * A_DIM
B_QKV = 3 * B_HEADS * B_DIM
IN_SPLITS = (A_QKV, B_QKV, B_HEADS * B_DIM, 2 * B_HEADS, 2 * B_HEADS, C_WIDTH, C_WIDTH, N_BRANCH * D_MODEL)

F32 = jnp.float32
BF16 = jnp.bfloat16
HIGHEST = lax.Precision.HIGHEST
NT_DIMS = (((1,), (1,)), ((), ()))
SUBLANES = 8
LANES = 128


def _matmul_kernel(x_ref, w_ref, o_ref):
    o_ref[...] = jnp.dot(x_ref[...].astype(BF16), w_ref[...].astype(BF16),
                         preferred_element_type=F32).astype(o_ref.dtype)


def matmul(x, w, *, tm=1024, tn=1024, out_dtype=F32):
    m, k = x.shape
    n = w.shape[1]
    tm, tn = min(tm, m), min(tn, n)
    assert m % tm == 0 and n % tn == 0, (m, n, tm, tn)
    return pl.pallas_call(
        _matmul_kernel,
        grid=(m // tm, n // tn),
        in_specs=[pl.BlockSpec((tm, k), lambda i, j: (i, 0)),
                  pl.BlockSpec((k, tn), lambda i, j: (0, j))],
        out_specs=pl.BlockSpec((tm, tn), lambda i, j: (i, j)),
        out_shape=jax.ShapeDtypeStruct((m, n), out_dtype),
        compiler_params=pltpu.CompilerParams(dimension_semantics=("parallel", "parallel")),
        name="matmul",
    )(x, w)


def matmul3(x, w, **kw):
    b, t, k = x.shape
    return matmul(x.reshape(b * t, k), w, **kw).reshape(b, t, -1)


def _diff_attn_kernel(lam_ref, q_ref, kt_ref, v_ref, g_ref, o_ref, *, out_scale):
    q = q_ref[0]
    kt = kt_ref[0]
    v = v_ref[0]
    lane = lax.broadcasted_iota(jnp.int32, q.shape, 1)
    zero = jnp.zeros_like(q)

    def softmax_v(qc):
        s = jnp.dot(qc, kt, preferred_element_type=F32)
        p = jnp.exp2(s - jnp.max(s, axis=-1, keepdims=True))
        l = jnp.sum(p, axis=-1, keepdims=True)
        return jnp.dot(p.astype(BF16), v, preferred_element_type=F32) / l

    o = softmax_v(jnp.where(lane < A_DIM, q, zero)) - lam_ref[0] * softmax_v(jnp.where(lane >= A_DIM, q, zero))
    y = o * lax.rsqrt(jnp.mean(o * o, axis=-1, keepdims=True) + EPS) * g_ref[...]
    o_ref[0] = y * out_scale


def diff_attention_core(q, kt, v, lam, sub_g, out_scale, *, tq=256):
    bn, tq_all, _ = q.shape
    tk = v.shape[1]
    tq = min(tq, tq_all)
    hd = 2 * A_DIM
    return pl.pallas_call(
        functools.partial(_diff_attn_kernel, out_scale=out_scale),
        grid=(bn, A_HEADS, tq_all // tq),
        in_specs=[pl.BlockSpec(memory_space=pltpu.SMEM),
                  pl.BlockSpec((1, tq, hd), lambda b, h, i: (b, i, h)),
                  pl.BlockSpec((1, hd, tk), lambda b, h, i: (b, h, 0)),
                  pl.BlockSpec((1, tk, hd), lambda b, h, i: (b, 0, h)),
                  pl.BlockSpec((1, hd), lambda b, h, i: (0, 0))],
        out_specs=pl.BlockSpec((1, tq, hd), lambda b, h, i: (b, i, h)),
        out_shape=jax.ShapeDtypeStruct((bn, tq_all, A_HEADS * hd), F32),
        compiler_params=pltpu.CompilerParams(dimension_semantics=("parallel", "parallel", "parallel")),
        name="diff_attn",
    )(lam.reshape(1).astype(F32), q, kt, v, sub_g.reshape(1, hd).astype(F32))


BMM = (((2,), (1,)), ((0,), (0,)))
BMM_NT = (((2,), (2,)), ((0,), (0,)))
BMM_TN = (((1,), (1,)), ((0,), (0,)))


def _bdot(a, b, dims=BMM):
    return lax.dot_general(a, b, dims, preferred_element_type=F32)


def _bdot3(a, b):
    a_hi, b_hi = a.astype(BF16), b.astype(BF16)
    a_lo = (a - a_hi.astype(F32)).astype(BF16)
    b_lo = (b - b_hi.astype(F32)).astype(BF16)
    return _bdot(a_hi, b_hi) + (_bdot(a_hi, b_lo) + _bdot(a_lo, b_hi))


def _gdn_kernel(q_ref, k_ref, v_ref, gcol_ref, bcol_ref, grow_ref, o_ref, s_ref, *, rev):
    @pl.when(pl.program_id(1) == 0)
    def _():
        s_ref[...] = jnp.zeros_like(s_ref)

    c = B_CHUNK
    ii = lax.broadcasted_iota(jnp.int32, (c, c), 0)
    jj = lax.broadcasted_iota(jnp.int32, (c, c), 1)
    incl = (ii <= jj) if rev else (ii >= jj)
    strict = (ii < jj) if rev else (ii > jj)
    eye = (ii == jj).astype(F32)
    m_incl = incl.astype(F32)
    last = 0 if rev else c - 1
    hs = range(B_HEADS)

    gc_cols = jnp.dot(m_incl, gcol_ref[0], precision=HIGHEST, preferred_element_type=F32)
    gc_rows = lax.dot_general(grow_ref[0, 0], m_incl, NT_DIMS, precision=HIGHEST, preferred_element_type=F32)
    beta_cols = bcol_ref[0]
    gi = jnp.stack([gc_cols[:, h:h + 1] for h in hs])
    gj = jnp.stack([gc_rows[h:h + 1, :] for h in hs])
    beta = jnp.stack([beta_cols[:, h:h + 1] for h in hs])
    heads = lambda ref: jnp.stack([ref[0, :, h * B_DIM:(h + 1) * B_DIM] for h in hs])
    q, k, v = heads(q_ref), heads(k_ref), heads(v_ref)

    dec = jnp.where(incl, jnp.exp(jnp.where(incl, gi - gj, 0.0)), 0.0)
    kb = k * beta
    kk = _bdot(kb, k, BMM_NT)
    qk = _bdot(q, k, BMM_NT) * dec
    lmat = jnp.where(strict, kk * dec, 0.0)
    same = lambda b: (ii // b) == (jj // b)
    base = 4
    nl = -jnp.where(same(base), lmat, 0.0)
    inv = eye + nl
    inv = inv + _bdot3(inv, _bdot3(nl, nl))
    b = base
    while b < c:
        join = jnp.where(same(2 * b) & jnp.logical_not(same(b)), lmat, 0.0)
        inv = inv - _bdot(_bdot(inv, join), inv)
        b *= 2
    eg = jnp.exp(gi)
    uw = _bdot(inv, jnp.concatenate([v * beta, kb * eg], axis=2))
    u, w = uw[:, :, :B_DIM], uw[:, :, B_DIM:]
    s = s_ref[...]
    v_new = u - _bdot(w, s)
    o = _bdot(q * eg, s) + _bdot(qk, v_new)
    for h in hs:
        o_ref[0, :, h * B_DIM:(h + 1) * B_DIM] = o[h]
    g_last = gi[:, last:last + 1, :]
    kd = k * jnp.exp(g_last - gi)
    s_ref[...] = s * jnp.exp(g_last) + _bdot(kd, v_new, BMM_TN)


def gdn_scan(q, k, v, g, beta, n_ctx, *, rev):
    bn, tn, _ = q.shape
    nc = tn // B_CHUNK
    ncc = n_ctx // B_CHUNK
    g_rows = jnp.swapaxes(g.reshape(bn, nc, B_CHUNK, B_HEADS), 2, 3)

    def chunk(n):
        return jnp.where(n < ncc, ncc - 1 - n, nc - 1 + ncc - n) if rev else n

    tok = lambda b, n: (b, chunk(n), 0)
    wide = pl.BlockSpec((1, B_CHUNK, B_HEADS * B_DIM), tok)
    narrow = pl.BlockSpec((1, B_CHUNK, B_HEADS), tok)
    return pl.pallas_call(
        functools.partial(_gdn_kernel, rev=rev),
        grid=(bn, nc),
        in_specs=[wide, wide, wide, narrow, narrow,
                  pl.BlockSpec((1, 1, B_HEADS, B_CHUNK), lambda b, n: (b, chunk(n), 0, 0))],
        out_specs=wide,
        out_shape=jax.ShapeDtypeStruct((bn, tn, B_HEADS * B_DIM), F32),
        scratch_shapes=[pltpu.VMEM((B_HEADS, B_DIM, B_DIM), F32)],
        compiler_params=pltpu.CompilerParams(dimension_semantics=("parallel", "arbitrary")),
        name="gdn_scan",
    )(q, k, v, g, beta, g_rows)


LRU_CG = 256
LRU_ROWS = 256


def _lru_tile_scan(a, b, h, row, rev):
    for s in (1, 2, 4):
        shift = SUBLANES - s if rev else s
        valid = (row < SUBLANES - s) if rev else (row >= s)
        a_s = pltpu.roll(a, shift, 0)
        b_s = pltpu.roll(b, shift, 0)
        b = jnp.where(valid, a * b_s + b, b)
        a = jnp.where(valid, a * a_s, a)
    ht = a * h + b
    return ht, (ht[0:1] if rev else ht[SUBLANES - 1:SUBLANES])


def _lru_kernel(x_ref, gate_ref, wr_ref, wi_ref, br_ref, bi_ref, c_ref, o_ref, *, n_ctx):
    tn = x_ref.shape[1]
    row = lax.broadcasted_iota(jnp.int32, (SUBLANES, LRU_CG), 0)
    ntile = LRU_ROWS // SUBLANES

    def chunk(r0, h, d, rev):
        xs = x_ref[0, pl.ds(r0, LRU_ROWS), :]
        xb = xs.astype(BF16)
        r = jax.nn.sigmoid(jnp.dot(xb, wr_ref[d, 0], preferred_element_type=F32) + br_ref[d])
        i = jax.nn.sigmoid(jnp.dot(xb, wi_ref[d, 0], preferred_element_type=F32) + bi_ref[d])
        log_a = c_ref[d] * r
        a = jnp.exp(log_a)
        b = jnp.sqrt(jnp.maximum(1.0 - a * a, 0.0)) * (i * xs)
        tiles = [None] * ntile
        for ti in (reversed(range(ntile)) if rev else range(ntile)):
            sl = slice(ti * SUBLANES, (ti + 1) * SUBLANES)
            tiles[ti], h = _lru_tile_scan(a[sl], b[sl], h, row, rev)
        hs = jnp.concatenate(tiles, axis=0)
        if d == 0:
            o_ref[0, pl.ds(r0, LRU_ROWS), :] = hs
        else:
            gate = gate_ref[0, pl.ds(r0, LRU_ROWS), :]
            o_ref[0, pl.ds(r0, LRU_ROWS), :] = (o_ref[0, pl.ds(r0, LRU_ROWS), :] + hs) * jax.nn.gelu(gate)
        return h

    for d, rev in ((0, False), (1, True)):
        h = jnp.zeros((1, LRU_CG), F32)
        for s0, s1 in ((0, n_ctx), (n_ctx, tn)):
            nch = (s1 - s0) // LRU_ROWS

            def body(ci, h, s0=s0, nch=nch, d=d, rev=rev):
                cidx = nch - 1 - ci if rev else ci
                return chunk(pl.multiple_of(s0 + cidx * LRU_ROWS, LRU_ROWS), h, d, rev)

            h = lax.fori_loop(0, nch, body, h)


def lru_scan(x, gate, w_r, b_r, w_i, b_i, lam, n_ctx):
    bn, tn, cw = x.shape
    assert n_ctx % LRU_ROWS == 0 and (tn - n_ctx) % LRU_ROWS == 0 and cw % LRU_CG == 0
    ncg = cw // LRU_CG
    per = LRU_CG // (C_WIDTH // C_BLOCKS)

    def blockdiag_tiles(w):
        blk = w.shape[-1]
        wt = w.astype(F32).reshape(2, ncg, per, blk, blk)
        eye = jnp.eye(per, dtype=F32)
        return jnp.einsum('dgpab,pq->dgpaqb', wt, eye).reshape(2, ncg, LRU_CG, LRU_CG).astype(BF16)

    cvec = (-LRU_C * jax.nn.softplus(-lam.astype(F32))).reshape(2, 1, cw)
    tok = pl.BlockSpec((1, tn, LRU_CG), lambda b, g: (b, 0, g))
    wspec = pl.BlockSpec((2, 1, LRU_CG, LRU_CG), lambda b, g: (0, g, 0, 0))
    vspec = pl.BlockSpec((2, 1, LRU_CG), lambda b, g: (0, 0, g))
    return pl.pallas_call(
        functools.partial(_lru_kernel, n_ctx=n_ctx),
        grid=(bn, ncg),
        in_specs=[tok, tok, wspec, wspec, vspec, vspec, vspec],
        out_specs=tok,
        out_shape=jax.ShapeDtypeStruct((bn, tn, cw), F32),
        compiler_params=pltpu.CompilerParams(dimension_semantics=("parallel", "parallel")),
        name="lru_scan",
    )(x, gate, blockdiag_tiles(w_r), blockdiag_tiles(w_i),
      b_r.astype(F32).reshape(2, 1, cw), b_i.astype(F32).reshape(2, 1, cw), cvec)


ROUTE_TOKENS = 128
PEER_PAIRS = P_HEADS * P_TOPK


def _sublane_topk(s, k, payload=None):
    g, width, tb = s.shape
    row = lax.broadcasted_iota(jnp.int32, s.shape, 1).astype(F32)
    out_row = lax.broadcasted_iota(jnp.int32, (g, k, tb), 1)
    vals = jnp.zeros((g, k, tb), F32)
    picked = jnp.zeros((g, k, tb), F32)
    for r in range(k):
        m = jnp.max(s, axis=1, keepdims=True)
        p = jnp.min(jnp.where(s == m, row, float(width)), axis=1, keepdims=True)
        sel = row == p
        if payload is not None:
            p = jnp.sum(jnp.where(sel, payload, 0.0), axis=1, keepdims=True)
        vals = jnp.where(out_row == r, m, vals)
        picked = jnp.where(out_row == r, p, picked)
        s = jnp.where(sel, -jnp.inf, s)
    return vals, picked


def _peer_route_kernel(q_ref, sk_ref, idx_ref, gate_ref):
    half = P_QDIM // 2

    def scores(c):
        out = []
        for h in range(P_HEADS):
            x = q_ref[:, (2 * h + c) * half:(2 * h + c + 1) * half]
            xn = x * lax.rsqrt(jnp.mean(x * x, axis=-1, keepdims=True) + EPS)
            out.append(lax.dot_general(sk_ref[h, c], xn, NT_DIMS, preferred_element_type=F32))
        return jnp.stack(out)

    v1, i1 = _sublane_topk(scores(0), P_TOPK)
    v2, i2 = _sublane_topk(scores(1), P_TOPK)
    cand = jnp.concatenate([v1[:, a:a + 1] + v2 for a in range(P_TOPK)], axis=1)
    cand_idx = jnp.concatenate([i1[:, a:a + 1] * float(P_KEYS) + i2 for a in range(P_TOPK)], axis=1)
    best, eid = _sublane_topk(cand, P_TOPK, payload=cand_idx)
    ex = jnp.exp(best - jnp.max(best, axis=1, keepdims=True))
    gate = ex / jnp.sum(ex, axis=1, keepdims=True)
    for h in range(P_HEADS):
        idx_ref[h * P_TOPK:(h + 1) * P_TOPK, :] = eid[h].astype(jnp.int32)
        gate_ref[h * P_TOPK:(h + 1) * P_TOPK, :] = gate[h]


def peer_route(q, subkeys):
    ntok = q.shape[0]
    assert ntok % ROUTE_TOKENS == 0
    out_block = pl.BlockSpec((PEER_PAIRS, ROUTE_TOKENS), lambda i: (0, i))
    idx_t, gate_t = pl.pallas_call(
        _peer_route_kernel,
        grid=(ntok // ROUTE_TOKENS,),
        in_specs=[pl.BlockSpec((ROUTE_TOKENS, q.shape[1]), lambda i: (i, 0)),
                  pl.BlockSpec(subkeys.shape, lambda i: (0, 0, 0, 0))],
        out_specs=[out_block, out_block],
        out_shape=[jax.ShapeDtypeStruct((PEER_PAIRS, ntok), jnp.int32),
                   jax.ShapeDtypeStruct((PEER_PAIRS, ntok), F32)],
        compiler_params=pltpu.CompilerParams(dimension_semantics=("parallel",)),
        name="peer_route",
    )(q, subkeys.astype(F32))
    return idx_t.T, gate_t.T


PEER_TOKENS = 8
PEER_LOOKAHEAD = 6


def _peer_expert_kernel(idx_cur, idx_nxt, h_ref, gate_ref, uv_hbm, o_ref, *scratch, nsteps):
    bufs, sem = scratch[:PEER_TOKENS], scratch[PEER_TOKENS]
    i = pl.program_id(0)
    slab, w = uv_hbm.shape[1:]
    half = slab // 2

    def issue(idx_ref, t, g, j):
        pltpu.make_async_copy(uv_hbm.at[idx_ref[t * PEER_PAIRS + g * SUBLANES + j]], bufs[t].at[:, g, j],
                              sem.at[t]).start(priority=j % 2)

    def wait(t):
        pltpu.make_async_copy(bufs[t], bufs[t], sem.at[t]).wait()

    @pl.when(i == 0)
    def _():
        for t in range(PEER_LOOKAHEAD):
            def body(g, carry, t=t):
                for j in range(SUBLANES):
                    issue(idx_cur, t, g, j)
                return carry
            lax.fori_loop(0, PEER_PAIRS // SUBLANES, body, 0)

    eye_t = (lax.broadcasted_iota(jnp.int32, (PEER_TOKENS, PEER_TOKENS), 0)
             == lax.broadcasted_iota(jnp.int32, (PEER_TOKENS, PEER_TOKENS), 1))
    tn_dims = (((0,), (0,)), ((), ()))
    hb_t = lax.dot_general(h_ref[...].astype(BF16), eye_t.astype(BF16), tn_dims,
                           preferred_element_type=F32).astype(BF16)
    gate_t = lax.dot_general(gate_ref[...], eye_t.astype(F32), tn_dims, precision=HIGHEST,
                             preferred_element_type=F32)
    tok = lax.broadcasted_iota(jnp.int32, gate_t.shape, 1)
    acc = jnp.zeros(o_ref.shape, F32)

    def half_rows(t, first):
        return jnp.concatenate([bufs[t][r].reshape(PEER_PAIRS, w).astype(BF16)
                                for r in range(first, first + half)], axis=1)

    def second_stage(t, s, acc):
        a = jnp.where(tok == t, jax.nn.gelu(s) * gate_t, 0.0)
        return acc + lax.dot_general(a.astype(BF16), half_rows(t, half), tn_dims, preferred_element_type=F32)

    s_prev = None
    for t in range(PEER_TOKENS):
        wait(t)
        ahead = t + PEER_LOOKAHEAD
        for k in range(PEER_PAIRS):
            issue(idx_cur if ahead < PEER_TOKENS else idx_nxt, ahead % PEER_TOKENS, k // SUBLANES, k % SUBLANES)
        s = jnp.dot(half_rows(t, 0), hb_t, preferred_element_type=F32)
        if t:
            acc = second_stage(t - 1, s_prev, acc)
        s_prev = s
    o_ref[...] = second_stage(PEER_TOKENS - 1, s_prev, acc)

    @pl.when(i == nsteps - 1)
    def _():
        for t in range(PEER_LOOKAHEAD):
            wait(t)


def peer_experts(h, idx, gate, u, v):
    ntok, dm = h.shape
    assert ntok % PEER_TOKENS == 0 and idx.shape == (ntok, PEER_PAIRS) and dm % (SUBLANES * LANES) == 0
    nsteps = ntok // PEER_TOKENS
    slab, w = 2 * dm // LANES, LANES
    uv = jnp.concatenate([u, v], axis=1).reshape(u.shape[0], slab, w)
    step_idx = PEER_TOKENS * PEER_PAIRS
    tok_block = lambda i: (i, 0)
    return pl.pallas_call(
        functools.partial(_peer_expert_kernel, nsteps=nsteps),
        grid=(nsteps,),
        in_specs=[pl.BlockSpec((step_idx,), lambda i: (i,), memory_space=pltpu.SMEM),
                  pl.BlockSpec((step_idx,), lambda i: (jnp.minimum(i + 1, nsteps - 1),), memory_space=pltpu.SMEM),
                  pl.BlockSpec((PEER_TOKENS, dm), tok_block),
                  pl.BlockSpec((PEER_TOKENS, PEER_PAIRS), tok_block),
                  pl.BlockSpec(memory_space=pl.ANY)],
        out_specs=pl.BlockSpec((PEER_TOKENS, dm), tok_block),
        out_shape=jax.ShapeDtypeStruct((ntok, dm), F32),
        scratch_shapes=[pltpu.VMEM((slab, PEER_PAIRS // SUBLANES, SUBLANES, w), F32)] * PEER_TOKENS
                       + [pltpu.SemaphoreType.DMA((PEER_TOKENS,))],
        compiler_params=pltpu.CompilerParams(dimension_semantics=("arbitrary",)),
        name="peer_experts",
    )(idx.reshape(-1), idx.reshape(-1), h, gate, uv)


def rmsnorm(x, g=None):
    y = x * lax.rsqrt(jnp.mean(x * x, axis=-1, keepdims=True) + EPS)
    return y if g is None else y * g.astype(F32)


def l2norm(x):
    return x * lax.rsqrt(jnp.sum(x * x, axis=-1, keepdims=True) + EPS)


def adaln(cond, w, b):
    n = cond.shape[0]
    cp = jnp.pad(jax.nn.silu(cond), ((0, (-n) % SUBLANES), (0, 0)))
    m = (matmul(cp, w)[:n] + b)[:, None, :]
    return jnp.split(m, N_MOD, axis=-1)


def dwconv(x, w, b=None):
    pad = SHORT_CONV // 2
    y = lax.conv_general_dilated(x, w[:, None, :].astype(x.dtype), window_strides=(1,),
                                 padding=[(pad, pad)], dimension_numbers=('NWC', 'WIO', 'NWC'),
                                 feature_group_count=x.shape[-1])
    return y if b is None else y + b.astype(x.dtype)


def segment_conv(x, n_ctx, w, b=None):
    return jnp.concatenate([dwconv(x[:, :n_ctx], w, b), dwconv(x[:, n_ctx:], w, b)], axis=1)


def axial_rope_tables(rows, dtype):
    row = jnp.repeat(jnp.arange(rows, dtype=F32), GRID_W)
    col = jnp.tile(jnp.arange(GRID_W, dtype=F32), rows)
    nf = A_DIM // 4
    inv = ROPE_THETA ** (-jnp.arange(nf, dtype=F32) / nf)
    ar, ac = row[:, None] * inv, col[:, None] * inv
    return tuple(t.astype(dtype) for t in (jnp.cos(ar), jnp.sin(ar), jnp.cos(ac), jnp.sin(ac)))


def rope_half(x, cos, sin):
    x1, x2 = jnp.split(x, 2, axis=-1)
    cs, sn = cos[:, None, None, :], sin[:, None, None, :]
    return jnp.concatenate([x1 * cs - x2 * sn, x2 * cs + x1 * sn], axis=-1)


def axial_rope(x, tabs):
    cr, sr, cc, sc = tabs
    xr, xc = jnp.split(x, 2, axis=-1)
    return jnp.concatenate([rope_half(xr, cr, sr), rope_half(xc, cc, sc)], axis=-1)


def diff_attention(qkv, n_ctx, rope, layer, need_ctx, qn_g, kn_g, lq1, lk1, lq2, lk2, sub_g):
    lam_init = 0.8 - 0.6 * math.exp(-0.3 * layer)
    f = lambda t: t.astype(F32)
    lam = jnp.exp(jnp.sum(f(lq1) * f(lk1))) - jnp.exp(jnp.sum(f(lq2) * f(lk2))) + lam_init
    q, k, v = jnp.split(qkv, 3, axis=-1)
    bn, tn, _ = q.shape

    def norm_rotate(t, g):
        t = rmsnorm(t.reshape(bn, tn, A_HEADS, 2, A_DIM), g)
        return jnp.concatenate([t[:, :n_ctx], axial_rope(t[:, n_ctx:], rope)], axis=1).reshape(bn, tn, -1)

    q = (norm_rotate(q, qn_g) * (A_DIM ** -0.5 * math.log2(math.e))).astype(BF16)
    kt = jnp.swapaxes(norm_rotate(k, kn_g).astype(BF16), 1, 2)
    v = v.astype(BF16)
    scale = 1.0 - lam_init
    yl = diff_attention_core(q[:, n_ctx:], kt, v, lam, sub_g, scale)
    if need_ctx:
        yc = diff_attention_core(q[:, :n_ctx], kt[:, :, :n_ctx], v[:, :n_ctx], lam, sub_g, scale)
    else:
        yc = jnp.zeros((bn, n_ctx, yl.shape[-1]), F32)
    return jnp.concatenate([yc, yl], axis=1)


def gated_deltanet(qkv, z, b_raw, a_raw, n_ctx, conv_w, a_log, dt_bias, norm_g):
    bn, tn, _ = qkv.shape
    qkv = jax.nn.silu(segment_conv(qkv, n_ctx, conv_w))
    q, k, v = jnp.split(qkv, 3, axis=-1)
    heads = lambda t: t.reshape(bn, tn, B_HEADS, B_DIM)
    q = (l2norm(heads(q)) * (B_DIM ** -0.5)).reshape(bn, tn, -1)
    k = l2norm(heads(k)).reshape(bn, tn, -1)
    beta = jax.nn.sigmoid(b_raw)
    g = -jnp.exp(a_log.astype(F32)).reshape(-1) * jax.nn.softplus(a_raw + dt_bias.astype(F32).reshape(-1))
    o = (gdn_scan(q, k, v, g[..., :B_HEADS], beta[..., :B_HEADS], n_ctx, rev=False)
         + gdn_scan(q, k, v, g[..., B_HEADS:], beta[..., B_HEADS:], n_ctx, rev=True))
    y = rmsnorm(heads(o), norm_g) * jax.nn.silu(heads(z))
    return y.reshape(bn, tn, -1)


def token_mixer(h, n_ctx, rope, layer, need_ctx, w_in,
                attn_qn_g, attn_kn_g, lam_q1, lam_k1, lam_q2, lam_k2, attn_sub_g,
                gdn_conv_w, gdn_a_log, gdn_dt_bias, gdn_norm_g,
                lru_conv_w, lru_conv_b, lru_w_r, lru_b_r, lru_w_i, lru_b_i, lru_lambda,
                w_branch, w_out):
    offs = np.cumsum((0,) + IN_SPLITS)
    big = [0, 1, 2, 5, 6, 7]
    w_big = jnp.concatenate([w_in[:, offs[i]:offs[i + 1]] for i in big], axis=1).astype(BF16)
    w_small = jnp.pad(w_in[:, offs[3]:offs[5]], ((0, 0), (0, LANES - 4 * B_HEADS))).astype(BF16)
    big_offs = np.cumsum([IN_SPLITS[i] for i in big])[:-1].tolist()
    p_attn, p_gdn, p_z, p_lru, p_lru_gate, p_merge = jnp.split(matmul3(h, w_big), big_offs, axis=-1)
    p_small = matmul3(h, w_small, tn=LANES)
    b_raw, a_raw = p_small[..., :2 * B_HEADS], p_small[..., 2 * B_HEADS:4 * B_HEADS]

    ya = diff_attention(p_attn, n_ctx, rope, layer, need_ctx, attn_qn_g, attn_kn_g,
                        lam_q1, lam_k1, lam_q2, lam_k2, attn_sub_g)
    yb = gated_deltanet(p_gdn, p_z, b_raw, a_raw, n_ctx, gdn_conv_w, gdn_a_log, gdn_dt_bias, gdn_norm_g)
    yr = lru_scan(segment_conv(p_lru, n_ctx, lru_conv_w, lru_conv_b), p_lru_gate,
                  lru_w_r, lru_b_r, lru_w_i, lru_b_i, lru_lambda, n_ctx)
    wb = w_branch.astype(BF16)
    gates = jnp.split(jax.nn.sigmoid(p_merge), N_BRANCH, axis=-1)
    y = gates[0] * matmul3(ya, wb[0])
    for j, yj in ((1, yb), (2, yr)):
        y = y + gates[j] * matmul3(yj, wb[j])
    return matmul3(y, w_out.astype(BF16))


def peer_ffn(h, wq, subkeys, u, v):
    idx, gate = peer_route(matmul(h, wq.astype(BF16)), subkeys)
    return peer_experts(h, idx, gate, u, v)


def kernel(x, c, ctx, c_ctx, w_ada, b_ada, norm1_g, norm2_g, w_in, attn_qn_g, attn_kn_g, lam_q1, lam_k1, lam_q2, lam_k2, attn_sub_g, gdn_conv_w, gdn_a_log, gdn_dt_bias, gdn_norm_g, lru_conv_w, lru_conv_b, lru_w_r, lru_b_r, lru_w_i, lru_b_i, lru_lambda, w_branch, w_out, peer_wq, peer_subkeys, peer_u, peer_v):
    bn, n_lat, dm = x.shape
    n_ctx = ctx.shape[1]
    rope = axial_rope_tables(n_lat // GRID_W, x.dtype)
    is_ctx = (jnp.arange(n_ctx + n_lat) < n_ctx)[None, :, None]
    xa = jnp.concatenate([ctx, x], axis=1)
    for l in range(DEPTH):
        need_ctx = l < DEPTH - 1
        w_ada_l = w_ada[l].astype(BF16)
        ml = adaln(c, w_ada_l, b_ada[l])
        mc = adaln(c_ctx[None, :], w_ada_l, b_ada[l])
        mod = [jnp.where(is_ctx, mc[j], ml[j]) for j in range(N_MOD)]
        h = rmsnorm(xa, norm1_g[l]) * (1.0 + mod[1]) + mod[0]
        y = token_mixer(h, n_ctx, rope, l, need_ctx, w_in[l],
                        attn_qn_g[l], attn_kn_g[l], lam_q1[l], lam_k1[l], lam_q2[l], lam_k2[l], attn_sub_g[l],
                        gdn_conv_w[l], gdn_a_log[l], gdn_dt_bias[l], gdn_norm_g[l],
                        lru_conv_w[l], lru_conv_b[l], lru_w_r[l], lru_b_r[l], lru_w_i[l], lru_b_i[l], lru_lambda[l],
                        w_branch[l], w_out[l])
        xa = xa + mod[2] * y
        if not need_ctx:
            xa, mod = xa[:, n_ctx:], [ml[j] for j in range(N_MOD)]
        h = rmsnorm(xa, norm2_g[l]) * (1.0 + mod[4]) + mod[3]
        f = peer_ffn(h.reshape(-1, dm), peer_wq[l], peer_subkeys[l], peer_u[l], peer_v[l])
        xa = xa + mod[5] * f.reshape(xa.shape)
    return xa
```
